```python
import math
import jax
import jax.numpy as jnp
from jax import lax
import numpy as np

D_MODEL = 1024
BATCH = 2
SEQ = 8192
DEPTH = 4
DEC_BATCH = 32
DEC_SEQ = 1
PAST_LEN = 8192
PAGE_SIZE = 128

N_MIXERS = 3
LAYER_KINDS = tuple(i % N_MIXERS for i in range(DEPTH))
N_GLA_LAYERS = sum(1 for kd in LAYER_KINDS if kd == 0)
DIFF_LAYER = 1

EPS = 1e-6
D_FF = 4 * D_MODEL

GLA_HEADS = 4
GLA_DK = D_MODEL // (2 * GLA_HEADS)
GLA_DV = D_MODEL // GLA_HEADS
GLA_QK = GLA_HEADS * GLA_DK
GLA_V = GLA_HEADS * GLA_DV
GLA_GATE_RANK = 16
GLA_TAU = 16.0
GLA_CHUNK = 64

ATTN_HEADS = 8
DIFF_DK = D_MODEL // (2 * ATTN_HEADS)
DIFF_DV = 2 * DIFF_DK
DIFF_LAMBDA_INIT = 0.8 - 0.6 * math.exp(-0.3 * DIFF_LAYER)
MOBA_DH = D_MODEL // ATTN_HEADS
MOBA_BLOCK = 256
MOBA_TOPK = 3
MOBA_Q_CHUNK = 32
ATTN_Q_BLOCK = 128

REL_BUCKETS = 32
REL_MAX_EXACT = REL_BUCKETS // 2
REL_MAX_DIST = 128

kernel_name = 'hybrid_gla_diffattn_moba_decode_step'


def rms_norm(x, g):
    xf = x.astype(jnp.float32)
    y = xf * lax.rsqrt(jnp.mean(xf * xf, axis=-1, keepdims=True) + EPS)
    return (y * g.astype(jnp.float32)).astype(x.dtype)


def squared_relu_mlp(h, w_up, w_down):
    a = jax.nn.relu(h @ w_up)
    return (a * a) @ w_down


def rel_bucket(dist):
    dist = jnp.maximum(dist, 0)
    is_small = dist < REL_MAX_EXACT
    df = jnp.maximum(dist, 1).astype(jnp.float32)
    large = REL_MAX_EXACT + (jnp.log(df / REL_MAX_EXACT) / math.log(REL_MAX_DIST / REL_MAX_EXACT)
                             * (REL_BUCKETS - REL_MAX_EXACT)).astype(jnp.int32)
    large = jnp.minimum(large, REL_BUCKETS - 1)
    return jnp.where(is_small, dist, large)


def gather_pages(cache, page_table):
    pages = cache[page_table]
    db, n_pages, ps = pages.shape[:3]
    return pages.reshape(db, n_pages * ps, pages.shape[3], pages.shape[4])


def gla_scan(q, k, v, log_a, s0):
    bsz, t, nh, _ = q.shape
    dv = v.shape[-1]
    c = min(GLA_CHUNK, t)
    pad = (-t) % c
    n = (t + pad) // c

    def prep(a):
        a = jnp.pad(a.astype(jnp.float32), ((0, 0), (0, pad), (0, 0), (0, 0)))
        return a.reshape(bsz, n, c, nh, a.shape[-1]).transpose(1, 0, 3, 2, 4)

    tri = jnp.tril(jnp.ones((c, c), dtype=bool))[:, :, None]

    def step(s, inp):
        qc, kc, vc, gc = inp
        b = jnp.cumsum(gc, axis=2)
        rel = b[:, :, :, None, :] - b[:, :, None, :, :]
        decay = jnp.exp(jnp.where(tri, rel, -jnp.inf))
        scores = jnp.einsum('bhid,bhjd,bhijd->bhij', qc, kc, decay)
        b_last = b[:, :, -1, :]
        o = (jnp.einsum('bhij,bhjv->bhiv', scores, vc)
             + jnp.einsum('bhid,bhdv->bhiv', qc * jnp.exp(b), s))
        s_new = (jnp.exp(b_last)[..., None] * s
                 + jnp.einsum('bhjd,bhjv->bhdv', kc * jnp.exp(b_last[:, :, None, :] - b), vc))
        return s_new, o

    s_fin, o = lax.scan(step, s0.astype(jnp.float32), (prep(q), prep(k), prep(v), prep(log_a)))
    o = o.transpose(1, 0, 3, 2, 4).reshape(bsz, n * c, nh, dv)[:, :t]
    return o, s_fin


def gla_mixer(h, s0, w_in, w_a1, w_a2, b_a, g_norm, w_out):
    bsz, t, _ = h.shape
    q, k, v, r = jnp.split(h @ w_in, [GLA_QK, 2 * GLA_QK, 2 * GLA_QK + GLA_V], axis=-1)
    q = q.reshape(bsz, t, GLA_HEADS, GLA_DK) * (GLA_DK ** -0.5)
    k = k.reshape(bsz, t, GLA_HEADS, GLA_DK)
    v = v.reshape(bsz, t, GLA_HEADS, GLA_DV)
    log_a = jax.nn.log_sigmoid(((h @ w_a1) @ w_a2 + b_a).astype(jnp.float32)) / GLA_TAU
    log_a = log_a.reshape(bsz, t, GLA_HEADS, GLA_DK)
    o, s_new = gla_scan(q, k, v, log_a, s0)
    o = rms_norm(o, g_norm).astype(h.dtype).reshape(bsz, t, GLA_V) * jax.nn.silu(r)
    return o @ w_out, s_new.astype(s0.dtype)


def diff_attend(q, k, v, q_pos, rel_bias, lam):
    bsz, tq, nh, _ = q.shape
    tk = k.shape[1]
    k1, k2 = k[..., :DIFF_DK], k[..., DIFF_DK:]
    blk = min(ATTN_Q_BLOCK, tq)
    pad = (-tq) % blk
    n = (tq + pad) // blk
    qb = jnp.pad(q, ((0, 0), (0, pad), (0, 0), (0, 0))).reshape(bsz, n, blk, nh, 2 * DIFF_DK)
    qb = qb.transpose(1, 0, 2, 3, 4)
    pb = jnp.pad(q_pos, (0, pad), mode='edge').reshape(n, blk)
    k_pos = jnp.arange(tk, dtype=jnp.int32)
    bias_ht = rel_bias.T.astype(jnp.float32)
    scale = DIFF_DK ** -0.5

    def one(args):
        qc, p = args
        dist = p[:, None] - k_pos[None, :]
        bias = bias_ht[:, rel_bucket(dist)][None]
        causal = dist >= 0

        def attn_map(qh, kh):
            s = jnp.einsum('bqhd,bkhd->bhqk', qh, kh).astype(jnp.float32) * scale + bias
            return jax.nn.softmax(jnp.where(causal, s, -jnp.inf), axis=-1)

        p_diff = attn_map(qc[..., :DIFF_DK], k1) - lam * attn_map(qc[..., DIFF_DK:], k2)
        return jnp.einsum('bhqk,bkhv->bqhv', p_diff.astype(v.dtype), v)

    out = lax.map(one, (qb, pb))
    return out.transpose(1, 0, 2, 3, 4).reshape(bsz, n * blk, nh, DIFF_DV)[:, :tq]


def diff_mixer(h, q_pos, k_past, v_past, rel_bias, w_in, q_norm, k_norm,
               lq1, lk1, lq2, lk2, subln, w_out):
    bsz, t, _ = h.shape
    q, k, v = jnp.split(h @ w_in, 3, axis=-1)
    q = rms_norm(q.reshape(bsz, t, ATTN_HEADS, 2, DIFF_DK), q_norm).reshape(bsz, t, ATTN_HEADS, 2 * DIFF_DK)
    k = rms_norm(k.reshape(bsz, t, ATTN_HEADS, 2, DIFF_DK), k_norm).reshape(bsz, t, ATTN_HEADS, 2 * DIFF_DK)
    v = v.reshape(bsz, t, ATTN_HEADS, DIFF_DV)
    if k_past is None:
        k_ctx, v_ctx = k, v
    else:
        k_ctx = jnp.concatenate([k_past.astype(k.dtype), k], axis=1)
        v_ctx = jnp.concatenate([v_past.astype(v.dtype), v], axis=1)
    lam = (jnp.exp(jnp.sum(lq1.astype(jnp.float32) * lk1.astype(jnp.float32)))
           - jnp.exp(jnp.sum(lq2.astype(jnp.float32) * lk2.astype(jnp.float32)))
           + DIFF_LAMBDA_INIT)
    o = diff_attend(q, k_ctx, v_ctx, q_pos, rel_bias, lam)
    o = rms_norm(o, subln) * (1.0 - DIFF_LAMBDA_INIT)
    return o.reshape(bsz, t, ATTN_HEADS * DIFF_DV) @ w_out, k, v


def moba_attend(q, k, v, q_pos, rel_bias):
    bsz, tq, nh, dh = q.shape
    tk = k.shape[1]
    kpad = (-tk) % MOBA_BLOCK
    nb = (tk + kpad) // MOBA_BLOCK
    kb = jnp.pad(k, ((0, 0), (0, kpad), (0, 0), (0, 0))).reshape(bsz, nb, MOBA_BLOCK, nh, dh).transpose(0, 3, 1, 2, 4)
    vb = jnp.pad(v, ((0, 0), (0, kpad), (0, 0), (0, 0))).reshape(bsz, nb, MOBA_BLOCK, nh, dh).transpose(0, 3, 1, 2, 4)
    k_mean = jnp.mean(kb.astype(jnp.float32), axis=3)
    n_top = min(MOBA_TOPK, nb)
    chunk = min(MOBA_Q_CHUNK, tq)
    pad = (-tq) % chunk
    n = (tq + pad) // chunk
    qb = jnp.pad(q, ((0, 0), (0, pad), (0, 0), (0, 0))).reshape(bsz, n, chunk, nh, dh).transpose(1, 0, 2, 3, 4)
    pb = jnp.pad(q_pos, (0, pad), mode='edge').reshape(n, chunk)
    bi = jnp.arange(bsz)[:, None, None, None]
    hi = jnp.arange(nh)[None, :, None, None]
    blk_ids = jnp.arange(nb, dtype=jnp.int32)
    offs = jnp.arange(MOBA_BLOCK, dtype=jnp.int32)
    bias_ht = rel_bias.T.astype(jnp.float32)
    scale = dh ** -0.5

    def one(args):
        qc, p = args
        own = p // MOBA_BLOCK
        gate = jnp.einsum('bqhd,bhnd->bhqn', qc.astype(jnp.float32), k_mean)
        fully_past = blk_ids[None, :] < own[:, None]
        gate = jnp.where(fully_past, gate, -jnp.inf)
        _, top = lax.top_k(gate, n_top)
        own_b = jnp.broadcast_to(own[None, None, :, None], (bsz, nh, chunk, 1)).astype(jnp.int32)
        sel = jnp.concatenate([top.astype(jnp.int32), own_b], axis=-1)
        sel_ok = jnp.concatenate([top < own_b, jnp.ones_like(own_b, dtype=bool)], axis=-1)
        kg = kb[bi, hi, sel]
        vg = vb[bi, hi, sel]
        dist = p[None, None, :, None, None] - (sel[..., None] * MOBA_BLOCK + offs)
        ok = sel_ok[..., None] & (dist >= 0)
        bias = bias_ht[hi[..., None], rel_bucket(dist)]
        s = jnp.einsum('bqhd,bhqskd->bhqsk', qc, kg).astype(jnp.float32) * scale + bias
        s = jnp.where(ok, s, -jnp.inf)
        probs = jax.nn.softmax(s.reshape(bsz, nh, chunk, -1), axis=-1).reshape(s.shape)
        return jnp.einsum('bhqsk,bhqskd->bqhd', probs.astype(v.dtype), vg)

    out = lax.map(one, (qb, pb))
    return out.transpose(1, 0, 2, 3, 4).reshape(bsz, n * chunk, nh, dh)[:, :tq]


def moba_mixer(h, q_pos, k_past, v_past, rel_bias, w_in, q_norm, k_norm, w_out):
    bsz, t, _ = h.shape
    q, k, v = jnp.split(h @ w_in, 3, axis=-1)
    q = rms_norm(q.reshape(bsz, t, ATTN_HEADS, MOBA_DH), q_norm)
    k = rms_norm(k.reshape(bsz, t, ATTN_HEADS, MOBA_DH), k_norm)
    v = v.reshape(bsz, t, ATTN_HEADS, MOBA_DH)
    if k_past is None:
        k_ctx, v_ctx = k, v
    else:
        k_ctx = jnp.concatenate([k_past.astype(k.dtype), k], axis=1)
        v_ctx = jnp.concatenate([v_past.astype(v.dtype), v], axis=1)
    o = moba_attend(q, k_ctx, v_ctx, q_pos, rel_bias)
    return o.reshape(bsz, t, ATTN_HEADS * MOBA_DH) @ w_out, k, v


def setup_inputs(seed: int = 0) -> dict:
    key = jax.random.key(seed)
    ks = iter(jax.random.split(key, 48))

    def nrm(shape, scale=1.0):
        return jax.random.normal(next(ks), shape, jnp.float32) * scale

    def gain(shape):
        return 1.0 + nrm(shape, 0.1)

    n_pages = PAST_LEN // PAGE_SIZE
    n_phys = (DEC_BATCH * n_pages * 5) // 4 + 1
    page_table = jax.random.permutation(next(ks), n_phys)[:DEC_BATCH * n_pages]
    page_table = page_table.reshape(DEC_BATCH, n_pages).astype(jnp.int32)
    d = D_MODEL
    g = N_GLA_LAYERS
    return {
        'x_prompt': nrm((BATCH, SEQ, d)),
        'x_sample': nrm((DEC_BATCH, DEC_SEQ, d)),
        'state_gla_l0': nrm((DEC_BATCH, GLA_HEADS, GLA_DK, GLA_DV)),
        'cache_k_l1': nrm((n_phys, PAGE_SIZE, ATTN_HEADS, 2 * DIFF_DK)),
        'cache_v_l1': nrm((n_phys, PAGE_SIZE, ATTN_HEADS, DIFF_DV)),
        'cache_k_l2': nrm((n_phys, PAGE_SIZE, ATTN_HEADS, MOBA_DH)),
        'cache_v_l2': nrm((n_phys, PAGE_SIZE, ATTN_HEADS, MOBA_DH)),
        'state_gla_l3': nrm((DEC_BATCH, GLA_HEADS, GLA_DK, GLA_DV)),
        'page_table': page_table,
        'rel_bias': nrm((REL_BUCKETS, ATTN_HEADS), 0.5),
        'norm_mix': gain((DEPTH, d)),
        'norm_mlp': gain((DEPTH, d)),
        'mlp_w_up': nrm((DEPTH, d, D_FF), d ** -0.5),
        'mlp_w_down': nrm((DEPTH, D_FF, d), D_FF ** -0.5),
        'gla_w_in': nrm((g, d, 2 * GLA_QK + 2 * GLA_V), d ** -0.5),
        'gla_w_a1': nrm((g, d, GLA_GATE_RANK), d ** -0.5),
        'gla_w_a2': nrm((g, GLA_GATE_RANK, GLA_QK), GLA_GATE_RANK ** -0.5),
        'gla_b_a': nrm((g, GLA_QK), 0.1),
        'gla_norm': gain((g, GLA_DV)),
        'gla_w_out': nrm((g, GLA_V, d), GLA_V ** -0.5),
        'diff_w_in': nrm((d, 3 * d), d ** -0.5),
        'diff_q_norm': gain((DIFF_DK,)),
        'diff_k_norm': gain((DIFF_DK,)),
        'diff_lq1': nrm((DIFF_DK,), 0.1),
        'diff_lk1': nrm((DIFF_DK,), 0.1),
        'diff_lq2': nrm((DIFF_DK,), 0.1),
        'diff_lk2': nrm((DIFF_DK,), 0.1),
        'diff_subln': gain((DIFF_DV,)),
        'diff_w_out': nrm((d, d), d ** -0.5),
        'moba_w_in': nrm((d, 3 * d), d ** -0.5),
        'moba_q_norm': gain((MOBA_DH,)),
        'moba_k_norm': gain((MOBA_DH,)),
        'moba_w_out': nrm((d, d), d ** -0.5),
    }


def reference(x_prompt, x_sample, state_gla_l0, cache_k_l1, cache_v_l1, cache_k_l2, cache_v_l2,
              state_gla_l3, page_table, rel_bias, norm_mix, norm_mlp, mlp_w_up, mlp_w_down,
              gla_w_in, gla_w_a1, gla_w_a2, gla_b_a, gla_norm, gla_w_out,
              diff_w_in, diff_q_norm, diff_k_norm, diff_lq1, diff_lk1, diff_lq2, diff_lk2,
              diff_subln, diff_w_out, moba_w_in, moba_q_norm, moba_k_norm, moba_w_out):
    bsz, t_p = x_prompt.shape[:2]
    t_s = x_sample.shape[1]
    past_len = page_table.shape[1] * PAGE_SIZE
    pos_p = jnp.arange(t_p, dtype=jnp.int32)
    pos_s = past_len + jnp.arange(t_s, dtype=jnp.int32)
    gla_states = (state_gla_l0, state_gla_l3)
    kv_caches = {1: (cache_k_l1, cache_v_l1), 2: (cache_k_l2, cache_v_l2)}

    xp, xs = x_prompt, x_sample
    new_state = []
    gla_idx = 0
    for i in range(DEPTH):
        kind = LAYER_KINDS[i]
        hp = rms_norm(xp, norm_mix[i])
        hs = rms_norm(xs, norm_mix[i])
        if kind == 0:
            gi = gla_idx
            w = (gla_w_in[gi], gla_w_a1[gi], gla_w_a2[gi], gla_b_a[gi], gla_norm[gi], gla_w_out[gi])
            s_in = gla_states[gi]
            s0_p = jnp.zeros((bsz, GLA_HEADS, GLA_DK, GLA_DV), s_in.dtype)
            mp, sp = gla_mixer(hp, s0_p, *w)
            ms, ss = gla_mixer(hs, s_in, *w)
            new_state += [sp, ss]
            gla_idx += 1
        elif kind == 1:
            k_past = gather_pages(kv_caches[i][0], page_table)
            v_past = gather_pages(kv_caches[i][1], page_table)
            w = (diff_w_in, diff_q_norm, diff_k_norm, diff_lq1, diff_lk1, diff_lq2, diff_lk2,
                 diff_subln, diff_w_out)
            mp, kp, vp = diff_mixer(hp, pos_p, None, None, rel_bias, *w)
            ms, k_s, v_s = diff_mixer(hs, pos_s, k_past, v_past, rel_bias, *w)
            new_state += [kp, vp, k_s, v_s]
        else:
            k_past = gather_pages(kv_caches[i][0], page_table)
            v_past = gather_pages(kv_caches[i][1], page_table)
            w = (moba_w_in, moba_q_norm, moba_k_norm, moba_w_out)
            mp, kp, vp = moba_mixer(hp, pos_p, None, None, rel_bias, *w)
            ms, k_s, v_s = moba_mixer(hs, pos_s, k_past, v_past, rel_bias, *w)
            new_state += [kp, vp, k_s, v_s]
        xp = xp + mp
        xs = xs + ms
        xp = xp + squared_relu_mlp(rms_norm(xp, norm_mlp[i]), mlp_w_up[i], mlp_w_down[i])
        xs = xs + squared_relu_mlp(rms_norm(xs, norm_mlp[i]), mlp_w_up[i], mlp_w_down[i])

    (st_p0, st_s0, k_p1, v_p1, k_s1, v_s1, k_p2, v_p2, k_s2, v_s2, st_p3, st_s3) = new_state
    return (xp, xs, st_p0, st_s0, k_p1, v_p1, k_s1, v_s1, k_p2, v_p2, k_s2, v_s2, st_p3, st_s3)
```

```python
import functools
import math

import numpy as np
import jax
import jax.numpy as jnp
from jax import lax
from jax.experimental import pallas as pl
from jax.experimental.pallas import tpu as pltpu

F32 = jnp.float32
BF16 = jnp.bfloat16
NEG_INF = float("-inf")
HIGHEST = lax.Precision.HIGHEST

EPS = 1e-6
D_MODEL = 1024
PAGE_SIZE = 128

GLA_HEADS = 4
GLA_DK = 128
GLA_DV = 256
GLA_QK = GLA_HEADS * GLA_DK
GLA_V = GLA_HEADS * GLA_DV
GLA_TAU = 16.0
GLA_RANK_PAD = 128
GLA_CHUNK = 128
GLA_SUB = 16

ATTN_HEADS = 8
HEAD_W = 128
DIFF_DK = 64
DIFF_LAMBDA_INIT = 0.8 - 0.6 * math.exp(-0.3 * 1)
MOBA_BLOCK = 256
MOBA_TOPK = 3
DIFF_TB = 512
MOBA_TB = MOBA_BLOCK

REL_BUCKETS = 32
REL_MAX_EXACT = 16
REL_MAX_DIST = 128
FAR_BUCKET = REL_BUCKETS - 1

V7X_VMEM_BYTES = 64 * 1024 * 1024
MIB = 1024 * 1024
DEC_PAGES_PER_STEP = 4

NT_DIMS = (((1,), (1,)), ((), ()))


def _cparams(sem, vmem_mib):
    assert vmem_mib * MIB < V7X_VMEM_BYTES
    return pltpu.CompilerParams(dimension_semantics=sem, vmem_limit_bytes=vmem_mib * MIB)


def _rel_bucket_np(dist):
    dist = np.maximum(dist, 0)
    df = np.maximum(dist, 1).astype(np.float32)
    large = REL_MAX_EXACT + (np.log(df / np.float32(REL_MAX_EXACT)) / np.float32(math.log(REL_MAX_DIST / REL_MAX_EXACT))
                             * np.float32(REL_BUCKETS - REL_MAX_EXACT)).astype(np.int32)
    large = np.minimum(large, REL_BUCKETS - 1)
    return np.where(dist < REL_MAX_EXACT, dist, large).astype(np.int32)


def _rms_norm(x, g):
    return x * lax.rsqrt(jnp.mean(x * x, axis=-1, keepdims=True) + EPS) * g


def _log_sigmoid(x):
    return jnp.minimum(x, 0.0) - jnp.log1p(jnp.exp(-jnp.abs(x)))


def _silu(x):
    return x * (1.0 / (1.0 + jnp.exp(-x)))


def _dot(a, b):
    return jnp.dot(a, b, preferred_element_type=F32)


def _dot_nt(a, b, precision=None):
    return lax.dot_general(a, b, NT_DIMS, precision=precision, preferred_element_type=F32)


def _row_tile(n, pref):
    t = min(n, pref)
    assert n % t == 0
    return t


def _gla_pre_kernel(x_ref, g_ref, win_ref, wa1_ref, wa2_ref, ba_ref,
                    q_ref, k_ref, v_ref, r_ref, la_ref):
    h = _rms_norm(x_ref[...], g_ref[...]).astype(BF16)
    y = _dot(h, win_ref[...])
    q_ref[...] = y[:, :GLA_QK] * (GLA_DK ** -0.5)
    k_ref[...] = y[:, GLA_QK:2 * GLA_QK]
    v_ref[...] = y[:, 2 * GLA_QK:2 * GLA_QK + GLA_V]
    r_ref[...] = y[:, 2 * GLA_QK + GLA_V:]
    low = _dot(h, wa1_ref[...]).astype(BF16)
    gate = _dot(low, wa2_ref[...]) + ba_ref[...]
    la_ref[...] = _log_sigmoid(gate) * (1.0 / GLA_TAU)


def _gla_pre(x, g, w_in, w_a1, w_a2, b_a, tm_pref=512):
    n, d = x.shape
    tm = _row_tile(n, tm_pref)
    rank = w_a1.shape[1]
    wa1 = jnp.pad(w_a1, ((0, 0), (0, GLA_RANK_PAD - rank))).astype(BF16)
    wa2 = jnp.pad(w_a2, ((0, GLA_RANK_PAD - rank), (0, 0))).astype(BF16)
    row = lambda i: (i, 0)
    full = lambda i: (0, 0)
    widths = (GLA_QK, GLA_QK, GLA_V, GLA_V, GLA_QK)
    return pl.pallas_call(
        _gla_pre_kernel,
        grid=(n // tm,),
        in_specs=[pl.BlockSpec((tm, d), row), pl.BlockSpec((1, d), full),
                  pl.BlockSpec(w_in.shape, full), pl.BlockSpec(wa1.shape, full),
                  pl.BlockSpec(wa2.shape, full), pl.BlockSpec((1, GLA_QK), full)],
        out_specs=[pl.BlockSpec((tm, w), row) for w in widths],
        out_shape=[jax.ShapeDtypeStruct((n, w), F32) for w in widths],
        compiler_params=_cparams(("parallel",), 48),
        name="gla_pre",
    )(x, g.reshape(1, d), w_in.astype(BF16), wa1, wa2, b_a.reshape(1, GLA_QK))


def _gla_scan_kernel(q_ref, k_ref, la_ref, v_ref, r_ref, gn_ref, s0_ref,
                     mo_ref, s_ref, b_scr):
    chunk = q_ref.shape[1]
    nsub = chunk // GLA_SUB

    @pl.when(pl.program_id(2) == 0)
    def _():
        s_ref[...] = s0_ref[...]

    q = q_ref[0]
    k = k_ref[0]
    v = v_ref[0]
    v_bf = v.astype(BF16)
    ri = lax.broadcasted_iota(jnp.int32, (chunk, chunk), 0)
    ci = lax.broadcasted_iota(jnp.int32, (chunk, chunk), 1)
    tri = (ri >= ci).astype(F32)
    b = jnp.dot(tri, la_ref[0], precision=HIGHEST, preferred_element_type=F32)
    b_scr[...] = b
    b_last = b[chunk - 1:chunk, :]

    s_old = s_ref[0, 0]
    o_inter = _dot((q * jnp.exp(b)).astype(BF16), s_old.astype(BF16))

    k_dec_t = (k * jnp.exp(b_last - b)).T
    decay_col = jnp.broadcast_to(jnp.exp(b_last), (GLA_DK, GLA_DK)).T
    decay = jnp.concatenate([decay_col] * (GLA_DV // GLA_DK), axis=1)
    s_ref[0, 0] = decay * s_old + _dot(k_dec_t.astype(BF16), v_bf)

    sub_row = lax.broadcasted_iota(jnp.int32, (GLA_SUB, GLA_DK), 0)
    gn = gn_ref[...]
    for blk in range(nsub):
        r0 = blk * GLA_SUB
        rows = slice(r0, r0 + GLA_SUB)
        q_i = q[rows]
        b_i = b[rows]
        acc = o_inter[rows]
        if blk > 0:
            ref_b = b_scr[r0 - 1:r0, :]
            q_t = (q_i * jnp.exp(b_i - ref_b)).astype(BF16)
            k_t = (k[:r0] * jnp.exp(ref_b - b[:r0])).astype(BF16)
            a_off = _dot_nt(q_t, k_t)
            acc = acc + _dot(a_off.astype(BF16), v_bf[:r0])
        for j in range(GLA_SUB):
            k_j = k_ref[0, r0 + j:r0 + j + 1, :]
            b_j = b_scr[r0 + j:r0 + j + 1, :]
            v_j = v_ref[0, r0 + j:r0 + j + 1, :]
            e = jnp.where(sub_row >= j, b_i - b_j, NEG_INF)
            w = jnp.sum(q_i * k_j * jnp.exp(e), axis=-1, keepdims=True)
            acc = acc + w * v_j
        o_n = _rms_norm(acc, gn) * _silu(r_ref[0, rows, :])
        mo_ref[0, rows, :] = o_n.astype(BF16)


def _gla_scan(q, k, la, v, r, g_norm, s0):
    bsz, t, _ = q.shape
    chunk = _row_tile(t, GLA_CHUNK)
    assert chunk % GLA_SUB == 0
    qk_spec = pl.BlockSpec((1, chunk, GLA_DK), lambda b, h, c: (b, c, h))
    v_spec = pl.BlockSpec((1, chunk, GLA_DV), lambda b, h, c: (b, c, h))
    s_spec = pl.BlockSpec((1, 1, GLA_DK, GLA_DV), lambda b, h, c: (b, h, 0, 0))
    return pl.pallas_call(
        _gla_scan_kernel,
        grid=(bsz, GLA_HEADS, t // chunk),
        in_specs=[qk_spec, qk_spec, qk_spec, v_spec, v_spec,
                  pl.BlockSpec((1, GLA_DV), lambda b, h, c: (0, 0)), s_spec],
        out_specs=[v_spec, s_spec],
        out_shape=[jax.ShapeDtypeStruct((bsz, t, GLA_V), BF16),
                   jax.ShapeDtypeStruct((bsz, GLA_HEADS, GLA_DK, GLA_DV), F32)],
        scratch_shapes=[pltpu.VMEM((chunk, GLA_DK), F32)],
        compiler_params=_cparams(("parallel", "parallel", "arbitrary"), 32),
        name="gla_scan",
    )(q, k, la, v, r, g_norm.reshape(1, GLA_DV), s0)


def _gla_step_kernel(q_ref, k_ref, la_ref, v_ref, r_ref, gn_ref, s0_ref, mo_ref, s_ref):
    gn = gn_ref[...]
    for h in range(GLA_HEADS):
        ks = slice(h * GLA_DK, (h + 1) * GLA_DK)
        vs = slice(h * GLA_DV, (h + 1) * GLA_DV)
        q = q_ref[0, :, ks]
        k = k_ref[0, :, ks]
        a = jnp.exp(la_ref[0, :, ks])
        v = v_ref[0, :, vs]
        s_old = s0_ref[0, h]

        def col(row_vec):
            sq = jnp.broadcast_to(row_vec, (GLA_DK, GLA_DK)).T
            return jnp.concatenate([sq] * (GLA_DV // GLA_DK), axis=1)

        s_ref[0, h] = col(a) * s_old + col(k) * v
        o = jnp.sum(col(q * a) * s_old, axis=0, keepdims=True) + jnp.sum(q * k, axis=-1, keepdims=True) * v
        o_n = _rms_norm(o, gn) * _silu(r_ref[0, :, vs])
        mo_ref[0, :, vs] = o_n.astype(BF16)


def _gla_step(q, k, la, v, r, g_norm, s0):
    n = q.shape[0]
    qk_spec = pl.BlockSpec((1, 1, GLA_QK), lambda b: (b, 0, 0))
    v_spec = pl.BlockSpec((1, 1, GLA_V), lambda b: (b, 0, 0))
    s_spec = pl.BlockSpec((1, GLA_HEADS, GLA_DK, GLA_DV), lambda b: (b, 0, 0, 0))
    r3 = lambda a: a.reshape(n, 1, a.shape[-1])
    mo, s_new = pl.pallas_call(
        _gla_step_kernel,
        grid=(n,),
        in_specs=[qk_spec, qk_spec, qk_spec, v_spec, v_spec,
                  pl.BlockSpec((1, GLA_DV), lambda b: (0, 0)), s_spec],
        out_specs=[v_spec, s_spec],
        out_shape=[jax.ShapeDtypeStruct((n, 1, GLA_V), BF16),
                   jax.ShapeDtypeStruct(s0.shape, F32)],
        compiler_params=_cparams(("parallel",), 16),
        name="gla_step",
    )(r3(q), r3(k), r3(la), r3(v), r3(r), g_norm.reshape(1, GLA_DV), s0)
    return mo.reshape(n, GLA_V), s_new


def _post_kernel(x_ref, mo_ref, wo_ref, g_ref, wup_ref, wdn_ref, y_ref, h_scr):
    @pl.when(pl.program_id(1) == 0)
    def _():
        x1 = x_ref[...] + _dot(mo_ref[...], wo_ref[...])
        y_ref[...] = x1
        h_scr[...] = _rms_norm(x1, g_ref[...]).astype(BF16)

    a = jnp.maximum(_dot(h_scr[...], wup_ref[...]), 0.0)
    y_ref[...] += _dot((a * a).astype(BF16), wdn_ref[...])


def _post(x, mo, w_out, g, w_up, w_down, tm_pref=512, tf_pref=2048):
    n, d = x.shape
    d_ff = w_up.shape[1]
    tm = _row_tile(n, tm_pref)
    tf = _row_tile(d_ff, tf_pref)
    return pl.pallas_call(
        _post_kernel,
        grid=(n // tm, d_ff // tf),
        in_specs=[pl.BlockSpec((tm, d), lambda i, j: (i, 0)),
                  pl.BlockSpec((tm, d), lambda i, j: (i, 0)),
                  pl.BlockSpec((d, d), lambda i, j: (0, 0)),
                  pl.BlockSpec((1, d), lambda i, j: (0, 0)),
                  pl.BlockSpec((d, tf), lambda i, j: (0, j)),
                  pl.BlockSpec((tf, d), lambda i, j: (j, 0))],
        out_specs=pl.BlockSpec((tm, d), lambda i, j: (i, 0)),
        out_shape=jax.ShapeDtypeStruct((n, d), F32),
        scratch_shapes=[pltpu.VMEM((tm, d), BF16)],
        compiler_params=_cparams(("parallel", "arbitrary"), 48),
        name="post_mlp",
    )(x, mo, w_out.astype(BF16), g.reshape(1, d), w_up.astype(BF16), w_down.astype(BF16))


def _seg_norm(x, g, seg):
    sq = x * x
    if seg == HEAD_W:
        scale = lax.rsqrt(jnp.mean(sq, axis=-1, keepdims=True) + EPS)
    else:
        assert 2 * seg == HEAD_W
        lo = lax.broadcasted_iota(jnp.int32, x.shape, 1) < seg
        s_lo = jnp.sum(jnp.where(lo, sq, 0.0), axis=-1, keepdims=True)
        s_hi = jnp.sum(jnp.where(lo, 0.0, sq), axis=-1, keepdims=True)
        scale = lax.rsqrt(jnp.where(lo, s_lo, s_hi) * (1.0 / seg) + EPS)
    return x * scale * g


def _attn_pre_kernel(x_ref, g_ref, win_ref, qn_ref, kn_ref, *out_refs, seg, q_scale, tb, emit_qf, emit_vt):
    out_refs = list(out_refs)
    qb_ref, k_ref, v_ref = out_refs[:3]
    qf_ref = out_refs[3] if emit_qf else None
    kb_ref, vt_ref = out_refs[-2:] if emit_vt else (None, None)
    d = x_ref.shape[1]
    h = _rms_norm(x_ref[...], g_ref[...]).astype(BF16)
    y = _dot(h, win_ref[...])
    tm = y.shape[0]
    for hd in range(ATTN_HEADS):
        sl = slice(hd * HEAD_W, (hd + 1) * HEAD_W)
        qh = _seg_norm(y[:, hd * HEAD_W:(hd + 1) * HEAD_W], qn_ref[:, sl], seg)
        kh = _seg_norm(y[:, d + hd * HEAD_W:d + (hd + 1) * HEAD_W], kn_ref[:, sl], seg)
        vh = y[:, 2 * d + hd * HEAD_W:2 * d + (hd + 1) * HEAD_W]
        qb_ref[:, sl] = (qh * q_scale).astype(BF16)
        k_ref[:, sl] = kh
        v_ref[:, sl] = vh
        if emit_qf:
            qf_ref[:, sl] = qh
        if emit_vt:
            kb_ref[:, sl] = kh.astype(BF16)
            for j in range(tm // tb):
                vt_ref[0, hd, j] = vh[j * tb:(j + 1) * tb].T.astype(BF16)


def _attn_pre(x, g, w_in, q_norm, k_norm, *, seg, q_scale, emit_qf=False, batch=None, tb=None, tm_pref=512):
    n, d = x.shape
    emit_vt = batch is not None
    tm = _row_tile(n, tm_pref)
    row = lambda i: (i, 0)
    full = lambda i: (0, 0)
    reps = d // q_norm.shape[0]
    dtypes = [BF16, F32, F32] + ([F32] if emit_qf else []) + ([BF16] if emit_vt else [])
    out_specs = [pl.BlockSpec((tm, d), row) for _ in dtypes]
    out_shape = [jax.ShapeDtypeStruct((n, d), dt) for dt in dtypes]
    if emit_vt:
        t = n // batch
        assert tm % tb == 0 and t % tm == 0
        tiles = t // tm
        out_specs.append(pl.BlockSpec((1, ATTN_HEADS, tm // tb, HEAD_W, tb),
                                      lambda i: (i // tiles, 0, i % tiles, 0, 0)))
        out_shape.append(jax.ShapeDtypeStruct((batch, ATTN_HEADS, t // tb, HEAD_W, tb), BF16))
    return pl.pallas_call(
        functools.partial(_attn_pre_kernel, seg=seg, q_scale=q_scale, tb=tb, emit_qf=emit_qf, emit_vt=emit_vt),
        grid=(n // tm,),
        in_specs=[pl.BlockSpec((tm, d), row), pl.BlockSpec((1, d), full),
                  pl.BlockSpec(w_in.shape, full), pl.BlockSpec((1, d), full), pl.BlockSpec((1, d), full)],
        out_specs=out_specs,
        out_shape=out_shape,
        compiler_params=_cparams(("parallel",), 56),
        name="attn_pre",
    )(x, g.reshape(1, d), w_in.astype(BF16),
      jnp.tile(q_norm, reps).reshape(1, d), jnp.tile(k_norm, reps).reshape(1, d))


def _bias_tile_kernel(bucket_ref, rb_ref, o_ref):
    h = pl.program_id(0)
    bucket = bucket_ref[...]
    far = rb_ref[FAR_BUCKET, h]
    acc = jnp.full(bucket.shape, NEG_INF, F32)
    for bk in range(REL_BUCKETS):
        acc = jnp.where(bucket == bk, rb_ref[bk, h] - far, acc)
    o_ref[0] = acc


def _bias_tiles(rel_bias, buckets):
    return pl.pallas_call(
        _bias_tile_kernel,
        grid=(ATTN_HEADS,),
        in_specs=[pl.BlockSpec(buckets.shape, lambda h: (0, 0, 0)),
                  pl.BlockSpec(memory_space=pltpu.SMEM)],
        out_specs=pl.BlockSpec((1,) + buckets.shape, lambda h: (h, 0, 0, 0)),
        out_shape=jax.ShapeDtypeStruct((ATTN_HEADS,) + buckets.shape, F32),
        compiler_params=_cparams(("parallel",), 32),
        name="bias_tiles",
    )(jnp.asarray(buckets), rel_bias)


def _block_buckets(tb):
    j = np.arange(tb)[:, None]
    i = np.arange(tb)[None, :]
    diag = np.where(i >= j, _rel_bucket_np(i - j), -1)
    prev = _rel_bucket_np(tb + i - j)
    return np.stack([diag, prev]).astype(np.int32)


def _softmax_step(s, vt, m_ref, l_ref, acc_ref):
    m_prev = m_ref[...]
    m_new = jnp.maximum(m_prev, jnp.max(s, axis=0, keepdims=True))
    alpha = jnp.exp(m_prev - m_new)
    p = jnp.exp(s - m_new)
    l_ref[...] = alpha * l_ref[...] + jnp.sum(p, axis=0, keepdims=True)
    acc_ref[...] = alpha * acc_ref[...] + _dot(vt, p.astype(BF16))
    m_ref[...] = m_new


def _diff_attn_kernel(q_ref, k_ref, vt_ref, bias_ref, lq1_ref, lk1_ref, lq2_ref, lk2_ref, subln_ref,
                      o_ref, m1, l1, a1, m2, l2, a2):
    tb = q_ref.shape[1]
    qi = pl.program_id(2)
    q = q_ref[0]
    lane = lax.broadcasted_iota(jnp.int32, q.shape, 1)
    q1 = jnp.where(lane < DIFF_DK, q, jnp.zeros_like(q))
    q2 = jnp.where(lane < DIFF_DK, jnp.zeros_like(q), q)

    for m_ref, l_ref, a_ref in ((m1, l1, a1), (m2, l2, a2)):
        m_ref[...] = jnp.full(m_ref.shape, NEG_INF, F32)
        l_ref[...] = jnp.zeros(l_ref.shape, F32)
        a_ref[...] = jnp.zeros(a_ref.shape, F32)

    def block(kb, bias):
        kblk = k_ref[0, pl.ds(pl.multiple_of(kb * tb, tb), tb), :]
        vt = vt_ref[0, 0, kb]
        s1 = _dot_nt(kblk, q1)
        s2 = _dot_nt(kblk, q2)
        if bias is not None:
            s1 = s1 + bias
            s2 = s2 + bias
        _softmax_step(s1, vt, m1, l1, a1)
        _softmax_step(s2, vt, m2, l2, a2)

    block(qi, bias_ref[0, 0])

    @pl.when(qi > 0)
    def _():
        block(qi - 1, bias_ref[0, 1])

    def far_block(kb, carry):
        block(kb, None)
        return carry

    lax.fori_loop(0, jnp.maximum(qi - 1, 0), far_block, 0)

    lam = (jnp.exp(jnp.sum(lq1_ref[...] * lk1_ref[...], axis=-1, keepdims=True))
           - jnp.exp(jnp.sum(lq2_ref[...] * lk2_ref[...], axis=-1, keepdims=True))
           + DIFF_LAMBDA_INIT)
    o_t = a1[...] * (1.0 / l1[...]) - lam * (a2[...] * (1.0 / l2[...]))
    o = _rms_norm(o_t.T, subln_ref[...]) * (1.0 - DIFF_LAMBDA_INIT)
    o_ref[0] = o.astype(BF16)


def _diff_attn(qb, kb, vt, bias, lq1, lk1, lq2, lk2, subln):
    bsz, t, d = qb.shape
    tb = vt.shape[-1]
    nq = t // tb
    vec = lambda a: a.reshape(1, -1)
    vspec = lambda w: pl.BlockSpec((1, w), lambda b, h, i: (0, 0))
    stat = pltpu.VMEM((1, tb), F32)
    acc = pltpu.VMEM((HEAD_W, tb), F32)
    return pl.pallas_call(
        _diff_attn_kernel,
        grid=(bsz, ATTN_HEADS, nq),
        in_specs=[pl.BlockSpec((1, tb, HEAD_W), lambda b, h, i: (b, i, h)),
                  pl.BlockSpec((1, t, HEAD_W), lambda b, h, i: (b, 0, h)),
                  pl.BlockSpec((1, 1, nq, HEAD_W, tb), lambda b, h, i: (b, h, 0, 0, 0)),
                  pl.BlockSpec((1, 2, tb, tb), lambda b, h, i: (h, 0, 0, 0)),
                  vspec(DIFF_DK), vspec(DIFF_DK), vspec(DIFF_DK), vspec(DIFF_DK), vspec(HEAD_W)],
        out_specs=pl.BlockSpec((1, tb, HEAD_W), lambda b, h, i: (b, i, h)),
        out_shape=jax.ShapeDtypeStruct((bsz, t, d), BF16),
        scratch_shapes=[stat, stat, acc, stat, stat, acc],
        compiler_params=_cparams(("parallel", "parallel", "arbitrary"), 48),
        name="diff_attn",
    )(qb, kb, vt, bias, vec(lq1), vec(lk1), vec(lq2), vec(lk2), vec(subln))


def _head_map_mask(rows, per_head_rows):
    r = lax.broadcasted_iota(jnp.int32, (rows, D_MODEL), 0)
    c = lax.broadcasted_iota(jnp.int32, (rows, D_MODEL), 1)
    col_head = jnp.right_shift(c, int(math.log2(HEAD_W)))
    row_head = jnp.right_shift(r, int(math.log2(per_head_rows)))
    return r, c, col_head == row_head


def _diff_decode_kernel(pt_ref, q_ref, ks_ref, vs_ref, blast_ref, bself_ref,
                        lq1_ref, lk1_ref, lq2_ref, lk2_ref, subln_ref, *rest):
    g = DEC_PAGES_PER_STEP
    k_pages = rest[:g]
    v_pages = rest[g:2 * g]
    o_ref, m_scr, l_scr, acc_scr = rest[2 * g:]
    step = pl.program_id(1)
    last = pl.num_programs(1) - 1
    nrow = 2 * ATTN_HEADS

    r, c, same_head = _head_map_mask(nrow, 2)
    row_map = jnp.bitwise_and(r, 1)
    own = same_head & (jnp.bitwise_and(jnp.right_shift(c, int(math.log2(DIFF_DK))), 1) == row_map)
    q_rows_f = jnp.where(own, jnp.broadcast_to(q_ref[0].astype(F32), (nrow, D_MODEL)), 0.0)
    q_rows = q_rows_f.astype(BF16)

    @pl.when(step == 0)
    def _():
        s_self = jnp.sum(q_rows_f * ks_ref[0], axis=-1, keepdims=True) + bself_ref[...]
        m_scr[...] = s_self
        l_scr[...] = jnp.ones(l_scr.shape, F32)
        acc_scr[...] = jnp.broadcast_to(vs_ref[0], acc_scr.shape)

    for i in range(g):
        s = _dot_nt(q_rows, k_pages[i][0].astype(BF16))
        if i == g - 1:
            s = s + jnp.where(step == last, blast_ref[...], 0.0)
        m_prev = m_scr[...]
        m_new = jnp.maximum(m_prev, jnp.max(s, axis=-1, keepdims=True))
        alpha = jnp.exp(m_prev - m_new)
        p = jnp.exp(s - m_new)
        l_scr[...] = alpha * l_scr[...] + jnp.sum(p, axis=-1, keepdims=True)
        acc_scr[...] = alpha * acc_scr[...] + _dot(p.astype(BF16), v_pages[i][0].astype(BF16))
        m_scr[...] = m_new

    @pl.when(step == last)
    def _():
        lam = (jnp.exp(jnp.sum(lq1_ref[...] * lk1_ref[...], axis=-1, keepdims=True))
               - jnp.exp(jnp.sum(lq2_ref[...] * lk2_ref[...], axis=-1, keepdims=True))
               + DIFF_LAMBDA_INIT)
        normed = acc_scr[...] * (1.0 / l_scr[...])
        o1 = jnp.sum(jnp.where(same_head & (row_map == 0), normed, 0.0), axis=0, keepdims=True)
        o2 = jnp.sum(jnp.where(same_head & (row_map == 1), normed, 0.0), axis=0, keepdims=True)
        o = o1 - lam * o2
        _, _, head_cols = _head_map_mask(ATTN_HEADS, 1)
        o8 = jnp.where(head_cols, jnp.broadcast_to(o, (ATTN_HEADS, D_MODEL)), 0.0)
        inv = lax.rsqrt(jnp.sum(o8 * o8, axis=-1, keepdims=True) * (1.0 / HEAD_W) + EPS)
        scale = jnp.sum(jnp.where(head_cols, inv, 0.0), axis=0, keepdims=True)
        o_ref[0] = (o * scale * subln_ref[...] * (1.0 - DIFF_LAMBDA_INIT)).astype(BF16)


def _diff_decode(page_table, qb, k_self, v_self, cache_k, cache_v, rel_bias, lq1, lk1, lq2, lk2, subln):
    n, d = qb.shape
    n_pages = page_table.shape[1]
    g = DEC_PAGES_PER_STEP
    assert n_pages % g == 0
    ck = cache_k.reshape(cache_k.shape[0], PAGE_SIZE, d)
    cv = cache_v.reshape(cache_v.shape[0], PAGE_SIZE, d)
    far = rel_bias[FAR_BUCKET]
    b_last = (rel_bias[_rel_bucket_np(PAGE_SIZE - np.arange(PAGE_SIZE))] - far).T
    b_last = jnp.repeat(b_last, 2, axis=0)
    b_self = jnp.repeat((rel_bias[0] - far).reshape(ATTN_HEADS, 1), 2, axis=0)
    r3 = lambda a: a.reshape(n, 1, d)
    vec = lambda a: a.reshape(1, -1)
    const = lambda shape: pl.BlockSpec(shape, lambda b, j, pt: (0,) * len(shape))
    seq = pl.BlockSpec((1, 1, d), lambda b, j, pt: (b, 0, 0))

    def page_spec(i):
        return pl.BlockSpec((1, PAGE_SIZE, d), lambda b, j, pt: (pt[b * n_pages + j * g + i], 0, 0))

    nrow = 2 * ATTN_HEADS
    out = pl.pallas_call(
        _diff_decode_kernel,
        grid_spec=pltpu.PrefetchScalarGridSpec(
            num_scalar_prefetch=1,
            grid=(n, n_pages // g),
            in_specs=[seq, seq, seq, const((nrow, PAGE_SIZE)), const((nrow, 1)),
                      const((1, DIFF_DK)), const((1, DIFF_DK)), const((1, DIFF_DK)), const((1, DIFF_DK)),
                      const((1, d))]
                     + [page_spec(i) for i in range(g)] + [page_spec(i) for i in range(g)],
            out_specs=seq,
            scratch_shapes=[pltpu.VMEM((nrow, 1), F32), pltpu.VMEM((nrow, 1), F32), pltpu.VMEM((nrow, d), F32)]),
        out_shape=jax.ShapeDtypeStruct((n, 1, d), BF16),
        compiler_params=_cparams(("parallel", "arbitrary"), 32),
        name="diff_decode",
    )(page_table.reshape(-1), r3(qb), r3(k_self), r3(v_self), b_last, b_self,
      vec(lq1), vec(lk1), vec(lq2), vec(lk2), vec(jnp.tile(subln, ATTN_HEADS)),
      *([ck] * g), *([cv] * g))
    return out.reshape(n, d)


def _block_mean_kernel(k_ref, o_ref):
    for i in range(o_ref.shape[0]):
        o_ref[i:i + 1, :] = jnp.mean(k_ref[i * MOBA_BLOCK:(i + 1) * MOBA_BLOCK, :], axis=0, keepdims=True)


def _block_mean(k):
    n, d = k.shape
    nb = n // MOBA_BLOCK
    per = _row_tile(nb, 8)
    return pl.pallas_call(
        _block_mean_kernel,
        grid=(nb // per,),
        in_specs=[pl.BlockSpec((per * MOBA_BLOCK, d), lambda i: (i, 0))],
        out_specs=pl.BlockSpec((per, d), lambda i: (i, 0)),
        out_shape=jax.ShapeDtypeStruct((nb, d), F32),
        compiler_params=_cparams(("parallel",), 32),
        name="moba_block_mean",
    )(k)


def _topk_rank(gate, blk_idx, nb):
    rank = jnp.zeros(gate.shape, F32)
    for m in range(nb):
        gm = gate[m:m + 1, :]
        beats = (gm > gate) | ((gm == gate) & (blk_idx > m))
        rank = rank + beats.astype(F32)
    return rank


def _moba_attn_kernel(qf_ref, qb_ref, k_ref, vt_ref, kmean_ref, bias_ref, o_ref, sel_scr, m_scr, l_scr, acc_scr):
    tb = qf_ref.shape[1]
    nb = kmean_ref.shape[1]
    qi = pl.program_id(2)
    q = qb_ref[0]

    gate = _dot_nt(kmean_ref[0], qf_ref[0], precision=HIGHEST)
    blk_idx = lax.broadcasted_iota(jnp.int32, gate.shape, 0)
    past = blk_idx < qi
    gate = jnp.where(past, gate, NEG_INF)
    rank = _topk_rank(gate, blk_idx, nb)
    sel_scr[...] = (past & (rank < MOBA_TOPK)).astype(F32)

    m_scr[...] = jnp.full(m_scr.shape, NEG_INF, F32)
    l_scr[...] = jnp.zeros(l_scr.shape, F32)
    acc_scr[...] = jnp.zeros(acc_scr.shape, F32)

    def block(kb, bias, masked):
        kblk = k_ref[0, pl.ds(pl.multiple_of(kb * tb, tb), tb), :]
        s = _dot_nt(kblk, q)
        if bias is not None:
            s = s + bias
        if masked:
            s = jnp.where(sel_scr[pl.ds(kb, 1), :] > 0.0, s, NEG_INF)
        _softmax_step(s, vt_ref[0, 0, kb], m_scr, l_scr, acc_scr)

    block(qi, bias_ref[0, 0], False)

    @pl.when(qi > 0)
    def _():
        block(qi - 1, bias_ref[0, 1], True)

    def far_block(kb, carry):
        block(kb, None, True)
        return carry

    lax.fori_loop(0, jnp.maximum(qi - 1, 0), far_block, 0)
    o_ref[0] = (acc_scr[...] * (1.0 / l_scr[...])).T.astype(BF16)


def _moba_attn(qf, qb, kb, vt, kmean, bias):
    bsz, t, d = qb.shape
    tb = vt.shape[-1]
    assert tb == MOBA_BLOCK
    nq = t // tb
    nb = kmean.shape[1]
    qspec = pl.BlockSpec((1, tb, HEAD_W), lambda b, h, i: (b, i, h))
    return pl.pallas_call(
        _moba_attn_kernel,
        grid=(bsz, ATTN_HEADS, nq),
        in_specs=[qspec, qspec,
                  pl.BlockSpec((1, t, HEAD_W), lambda b, h, i: (b, 0, h)),
                  pl.BlockSpec((1, 1, nq, HEAD_W, tb), lambda b, h, i: (b, h, 0, 0, 0)),
                  pl.BlockSpec((1, nb, HEAD_W), lambda b, h, i: (b, 0, h)),
                  pl.BlockSpec((1, 2, tb, tb), lambda b, h, i: (h, 0, 0, 0))],
        out_specs=qspec,
        out_shape=jax.ShapeDtypeStruct((bsz, t, d), BF16),
        scratch_shapes=[pltpu.VMEM((nb, tb), F32), pltpu.VMEM((1, tb), F32), pltpu.VMEM((1, tb), F32),
                        pltpu.VMEM((HEAD_W, tb), F32)],
        compiler_params=_cparams(("parallel", "parallel", "arbitrary"), 32),
        name="moba_attn",
    )(qf, qb, kb, vt, kmean, bias)


def _moba_route_kernel(pt_ref, qf_ref, *rest):
    g = DEC_PAGES_PER_STEP
    k_pages = rest[:g]
    sel_ref, ksum_scr = rest[g:]
    step = pl.program_id(1)
    pages_per_block = MOBA_BLOCK // PAGE_SIZE
    blocks_per_step = g // pages_per_block
    nb = ksum_scr.shape[0]
    for i in range(blocks_per_step):
        tot = jnp.sum(k_pages[i * pages_per_block][0], axis=0, keepdims=True)
        for p in range(1, pages_per_block):
            tot = tot + jnp.sum(k_pages[i * pages_per_block + p][0], axis=0, keepdims=True)
        ksum_scr[pl.ds(step * blocks_per_step + i, 1), :] = tot * (1.0 / MOBA_BLOCK)

    @pl.when(step == pl.num_programs(1) - 1)
    def _():
        rows = sel_ref.shape[2]
        _, _, head_cols = _head_map_mask(rows, 1)
        q_rows = jnp.where(head_cols, jnp.broadcast_to(qf_ref[0], (rows, D_MODEL)), 0.0)
        gate = _dot_nt(ksum_scr[...], q_rows, precision=HIGHEST)
        blk_idx = lax.broadcasted_iota(jnp.int32, gate.shape, 0)
        rank = _topk_rank(gate, blk_idx, nb)
        sel_ref[0] = jnp.zeros(sel_ref.shape[1:], jnp.int32)
        for t in range(MOBA_TOPK):
            sel_ref[0, t:t + 1, :] = jnp.sum(jnp.where(rank == t, blk_idx, 0), axis=0, keepdims=True)


def _moba_route(page_table, qf, cache_k):
    n, d = qf.shape
    n_pages = page_table.shape[1]
    g = DEC_PAGES_PER_STEP
    nb = n_pages * PAGE_SIZE // MOBA_BLOCK
    assert n_pages % g == 0 and g % (MOBA_BLOCK // PAGE_SIZE) == 0 and nb >= MOBA_TOPK
    ck = cache_k.reshape(cache_k.shape[0], PAGE_SIZE, d)

    def page_spec(i):
        return pl.BlockSpec((1, PAGE_SIZE, d), lambda b, j, pt: (pt[b * n_pages + j * g + i], 0, 0))

    return pl.pallas_call(
        _moba_route_kernel,
        grid_spec=pltpu.PrefetchScalarGridSpec(
            num_scalar_prefetch=1,
            grid=(n, n_pages // g),
            in_specs=[pl.BlockSpec((1, 1, d), lambda b, j, pt: (b, 0, 0))] + [page_spec(i) for i in range(g)],
            out_specs=pl.BlockSpec((1, 8, HEAD_W), lambda b, j, pt: (b, 0, 0)),
            scratch_shapes=[pltpu.VMEM((nb, d), F32)]),
        out_shape=jax.ShapeDtypeStruct((n, 8, HEAD_W), jnp.int32),
        compiler_params=_cparams(("parallel", "arbitrary"), 32),
        name="moba_route",
    )(page_table.reshape(-1), qf.reshape(n, 1, d), *([ck] * g))


def _moba_decode_kernel(pt_ref, sel_ref, qb_ref, ks_ref, vs_ref, bias_ref, *rest, newest_block):
    pages_per_block = MOBA_BLOCK // PAGE_SIZE
    n_sel = MOBA_TOPK * pages_per_block
    k_pages = rest[:n_sel]
    v_pages = rest[n_sel:2 * n_sel]
    o_ref = rest[2 * n_sel]
    b = pl.program_id(0)
    h = pl.program_id(1)
    q = qb_ref[0]
    q8 = jnp.broadcast_to(q, (8, HEAD_W))
    s_self = jnp.sum(q.astype(F32) * ks_ref[0], axis=-1, keepdims=True) + bias_ref[0, 1:2, 0:1]
    scores = []
    m = s_self
    for t in range(MOBA_TOPK):
        blk = sel_ref[(b * MOBA_TOPK + t) * ATTN_HEADS + h]
        for p in range(pages_per_block):
            s = _dot_nt(q8, k_pages[t * pages_per_block + p][0].astype(BF16))[0:1]
            if p == pages_per_block - 1:
                s = s + jnp.where(blk == newest_block, bias_ref[0, 0:1, :], 0.0)
            scores.append(s)
            m = jnp.maximum(m, jnp.max(s, axis=-1, keepdims=True))
    p_self = jnp.exp(s_self - m)
    l = p_self
    o = p_self * vs_ref[0]
    for i, s in enumerate(scores):
        p = jnp.exp(s - m)
        l = l + jnp.sum(p, axis=-1, keepdims=True)
        p8 = jnp.broadcast_to(p, (8, PAGE_SIZE)).astype(BF16)
        o = o + _dot(p8, v_pages[i][0].astype(BF16))[0:1]
    o_ref[0] = (o * (1.0 / l)).astype(BF16)


def _moba_decode(page_table, sel, qb, k_self, v_self, cache_k, cache_v, rel_bias):
    n, d = qb.shape
    n_pages = page_table.shape[1]
    pages_per_block = MOBA_BLOCK // PAGE_SIZE
    past = n_pages * PAGE_SIZE
    ck = cache_k.reshape(cache_k.shape[0], PAGE_SIZE, d)
    cv = cache_v.reshape(cache_v.shape[0], PAGE_SIZE, d)
    far = rel_bias[FAR_BUCKET]
    b_last = (rel_bias[_rel_bucket_np(PAGE_SIZE - np.arange(PAGE_SIZE))] - far).T
    b_self = jnp.broadcast_to((rel_bias[0] - far)[:, None], (ATTN_HEADS, PAGE_SIZE))
    bias = jnp.stack([b_last, b_self], axis=1)
    r3 = lambda a: a.reshape(n, 1, d)
    head = pl.BlockSpec((1, 1, HEAD_W), lambda b, h, pt, sl: (b, 0, h))

    def page_spec(t, p):
        def index(b, h, pt, sl):
            blk = sl[(b * MOBA_TOPK + t) * ATTN_HEADS + h]
            return (pt[b * n_pages + blk * pages_per_block + p], 0, h)
        return pl.BlockSpec((1, PAGE_SIZE, HEAD_W), index)

    pages = [page_spec(t, p) for t in range(MOBA_TOPK) for p in range(pages_per_block)]
    out = pl.pallas_call(
        functools.partial(_moba_decode_kernel, newest_block=past // MOBA_BLOCK - 1),
        grid_spec=pltpu.PrefetchScalarGridSpec(
            num_scalar_prefetch=2,
            grid=(n, ATTN_HEADS),
            in_specs=[head, head, head, pl.BlockSpec((1, 2, PAGE_SIZE), lambda b, h, pt, sl: (h, 0, 0))]
                     + pages + pages,
            out_specs=head),
        out_shape=jax.ShapeDtypeStruct((n, 1, d), BF16),
        compiler_params=_cparams(("parallel", "parallel"), 16),
        name="moba_decode",
    )(page_table.reshape(-1), sel[:, :MOBA_TOPK, :ATTN_HEADS].reshape(-1),
      r3(qb), r3(k_self), r3(v_self), bias, *([ck] * len(pages)), *([cv] * len(pages)))
    return out.reshape(n, d)


def kernel(x_prompt, x_sample, state_gla_l0, cache_k_l1, cache_v_l1, cache_k_l2, cache_v_l2, state_gla_l3, page_table, rel_bias, norm_mix, norm_mlp, mlp_w_up, mlp_w_down, gla_w_in, gla_w_a1, gla_w_a2, gla_b_a, gla_norm, gla_w_out, diff_w_in, diff_q_norm, diff_k_norm, diff_lq1, diff_lk1, diff_lq2, diff_lk2, diff_subln, diff_w_out, moba_w_in, moba_q_norm, moba_k_norm, moba_w_out):
    bsz, t, d = x_prompt.shape
    n_dec = x_sample.shape[0]
    assert d == D_MODEL and x_sample.shape[1] == 1
    xp = x_prompt.reshape(bsz * t, d)
    xs = x_sample.reshape(n_dec, d)
    seq = lambda a: a.reshape(bsz, t, a.shape[-1])
    heads = lambda a: a.reshape(a.shape[:-1] + (ATTN_HEADS, HEAD_W))

    def mlp(x, mo, w_out, li):
        return _post(x, mo, w_out, norm_mlp[li], mlp_w_up[li], mlp_w_down[li])

    def gla_layer(xp, xs, li, gi, state_dec):
        w = (gla_w_in[gi], gla_w_a1[gi], gla_w_a2[gi], gla_b_a[gi])
        q, k, v, r, la = _gla_pre(xp, norm_mix[li], *w)
        s0 = jnp.zeros((bsz,) + state_dec.shape[1:], F32)
        mo_p, st_p = _gla_scan(seq(q), seq(k), seq(la), seq(v), seq(r), gla_norm[gi], s0)
        q, k, v, r, la = _gla_pre(xs, norm_mix[li], *w)
        mo_s, st_s = _gla_step(q, k, la, v, r, gla_norm[gi], state_dec)
        xp = mlp(xp, mo_p.reshape(bsz * t, d), gla_w_out[gi], li)
        xs = mlp(xs, mo_s, gla_w_out[gi], li)
        return xp, xs, st_p, st_s

    xp, xs, st_p0, st_s0 = gla_layer(xp, xs, 0, 0, state_gla_l0)

    tb = min(DIFF_TB, t)
    lam_w = (diff_lq1, diff_lk1, diff_lq2, diff_lk2, diff_subln)
    pre = dict(seg=DIFF_DK, q_scale=DIFF_DK ** -0.5)
    qb, k, v, kb, vt = _attn_pre(xp, norm_mix[1], diff_w_in, diff_q_norm, diff_k_norm, batch=bsz, tb=tb, **pre)
    bias = _bias_tiles(rel_bias, _block_buckets(tb))
    mo_p = _diff_attn(seq(qb), seq(kb), vt, bias, *lam_w)
    k_p1, v_p1 = heads(seq(k)), heads(seq(v))
    qb, k, v = _attn_pre(xs, norm_mix[1], diff_w_in, diff_q_norm, diff_k_norm, **pre)
    mo_s = _diff_decode(page_table, qb, k, v, cache_k_l1, cache_v_l1, rel_bias, *lam_w)
    k_s1, v_s1 = heads(k.reshape(n_dec, 1, d)), heads(v.reshape(n_dec, 1, d))
    xp = mlp(xp, mo_p.reshape(bsz * t, d), diff_w_out, 1)
    xs = mlp(xs, mo_s, diff_w_out, 1)

    pre = dict(seg=HEAD_W, q_scale=HEAD_W ** -0.5, emit_qf=True)
    qb, k, v, qf, kb, vt = _attn_pre(xp, norm_mix[2], moba_w_in, moba_q_norm, moba_k_norm, batch=bsz, tb=MOBA_TB, **pre)
    kmean = _block_mean(k).reshape(bsz, t // MOBA_BLOCK, d)
    bias = _bias_tiles(rel_bias, _block_buckets(MOBA_TB))
    mo_p = _moba_attn(seq(qf), seq(qb), seq(kb), vt, kmean, bias)
    k_p2, v_p2 = heads(seq(k)), heads(seq(v))
    qb, k, v, qf = _attn_pre(xs, norm_mix[2], moba_w_in, moba_q_norm, moba_k_norm, **pre)
    sel = _moba_route(page_table, qf, cache_k_l2)
    mo_s = _moba_decode(page_table, sel, qb, k, v, cache_k_l2, cache_v_l2, rel_bias)
    k_s2, v_s2 = heads(k.reshape(n_dec, 1, d)), heads(v.reshape(n_dec, 1, d))
    xp = mlp(xp, mo_p.reshape(bsz * t, d), moba_w_out, 2)
    xs = mlp(xs, mo_s, moba_w_out, 2)

    xp, xs, st_p3, st_s3 = gla_layer(xp, xs, 3, 1, state_gla_l3)

    return (xp.reshape(bsz, t, d), xs.reshape(n_dec, 1, d), st_p0, st_s0,
            k_p1, v_p1, k_s1, v_s1, k_p2, v_p2, k_s2, v_s2, st_p3, st_s3)
```

```python
import functools
import math

import numpy as np
import jax
import jax.numpy as jnp
from jax import lax
from jax.experimental import pallas as pl
from jax.experimental.pallas import tpu as pltpu

F32 = jnp.float32
BF16 = jnp.bfloat16
NEG_INF = float("-inf")
HIGHEST = lax.Precision.HIGHEST

EPS = 1e-6
D_MODEL = 1024
PAGE_SIZE = 128

GLA_HEADS = 4
GLA_DK = 128
GLA_DV = 256
GLA_QK = GLA_HEADS * GLA_DK
GLA_V = GLA_HEADS * GLA_DV
GLA_TAU = 16.0
GLA_RANK_PAD = 128
GLA_CHUNK = 128
GLA_SUB = 16

ATTN_HEADS = 8
HEAD_W = 128
DIFF_DK = 64
DIFF_LAMBDA_INIT = 0.8 - 0.6 * math.exp(-0.3 * 1)
MOBA_BLOCK = 256
MOBA_TOPK = 3
DIFF_TB = 512
MOBA_TB = MOBA_BLOCK
DIFF_FAR_GROUP = 2
MOBA_FAR_GROUP = 4
LOG2E = math.log2(math.e)

REL_BUCKETS = 32
REL_MAX_EXACT = 16
REL_MAX_DIST = 128
FAR_BUCKET = REL_BUCKETS - 1

V7X_VMEM_BYTES = 64 * 1024 * 1024
MIB = 1024 * 1024
DEC_PAGES_PER_STEP = 4

NT_DIMS = (((1,), (1,)), ((), ()))


def _cparams(sem, vmem_mib):
    assert vmem_mib * MIB < V7X_VMEM_BYTES
    return pltpu.CompilerParams(dimension_semantics=sem, vmem_limit_bytes=vmem_mib * MIB)


def _rel_bucket_np(dist):
    dist = np.maximum(dist, 0)
    df = np.maximum(dist, 1).astype(np.float32)
    large = REL_MAX_EXACT + (np.log(df / np.float32(REL_MAX_EXACT)) / np.float32(math.log(REL_MAX_DIST / REL_MAX_EXACT))
                             * np.float32(REL_BUCKETS - REL_MAX_EXACT)).astype(np.int32)
    large = np.minimum(large, REL_BUCKETS - 1)
    return np.where(dist < REL_MAX_EXACT, dist, large).astype(np.int32)


def _rms_norm(x, g):
    return x * lax.rsqrt(jnp.mean(x * x, axis=-1, keepdims=True) + EPS) * g


def _log_sigmoid(x):
    return jnp.minimum(x, 0.0) - jnp.log1p(jnp.exp(-jnp.abs(x)))


def _silu(x):
    return x * (1.0 / (1.0 + jnp.exp(-x)))


def _dot(a, b):
    return jnp.dot(a, b, preferred_element_type=F32)


def _dot_nt(a, b, precision=None):
    return lax.dot_general(a, b, NT_DIMS, precision=precision, preferred_element_type=F32)


def _lambda_full(lq1_ref, lk1_ref, lq2_ref, lk2_ref):
    return (jnp.exp(jnp.sum(lq1_ref[...] * lk1_ref[...], axis=-1, keepdims=True))
            - jnp.exp(jnp.sum(lq2_ref[...] * lk2_ref[...], axis=-1, keepdims=True))
            + DIFF_LAMBDA_INIT)


def _row_tile(n, pref):
    t = min(n, pref)
    assert n % t == 0
    return t


def _gla_pre_kernel(x_ref, g_ref, win_ref, wa1_ref, wa2_ref, ba_ref,
                    q_ref, k_ref, v_ref, r_ref, la_ref):
    h = _rms_norm(x_ref[...], g_ref[...]).astype(BF16)
    y = _dot(h, win_ref[...])
    q_ref[...] = y[:, :GLA_QK] * (GLA_DK ** -0.5)
    k_ref[...] = y[:, GLA_QK:2 * GLA_QK]
    v_ref[...] = y[:, 2 * GLA_QK:2 * GLA_QK + GLA_V]
    r_ref[...] = y[:, 2 * GLA_QK + GLA_V:]
    low = _dot(h, wa1_ref[...]).astype(BF16)
    gate = _dot(low, wa2_ref[...]) + ba_ref[...]
    la_ref[...] = _log_sigmoid(gate) * (1.0 / GLA_TAU)


def _gla_pre(x, g, w_in, w_a1, w_a2, b_a, tm_pref=512):
    n, d = x.shape
    tm = _row_tile(n, tm_pref)
    rank = w_a1.shape[1]
    wa1 = jnp.pad(w_a1, ((0, 0), (0, GLA_RANK_PAD - rank))).astype(BF16)
    wa2 = jnp.pad(w_a2, ((0, GLA_RANK_PAD - rank), (0, 0))).astype(BF16)
    row = lambda i: (i, 0)
    full = lambda i: (0, 0)
    widths = (GLA_QK, GLA_QK, GLA_V, GLA_V, GLA_QK)
    return pl.pallas_call(
        _gla_pre_kernel,
        grid=(n // tm,),
        in_specs=[pl.BlockSpec((tm, d), row), pl.BlockSpec((1, d), full),
                  pl.BlockSpec(w_in.shape, full), pl.BlockSpec(wa1.shape, full),
                  pl.BlockSpec(wa2.shape, full), pl.BlockSpec((1, GLA_QK), full)],
        out_specs=[pl.BlockSpec((tm, w), row) for w in widths],
        out_shape=[jax.ShapeDtypeStruct((n, w), F32) for w in widths],
        compiler_params=_cparams(("parallel",), 48),
        name="gla_pre",
    )(x, g.reshape(1, d), w_in.astype(BF16), wa1, wa2, b_a.reshape(1, GLA_QK))


def _gla_scan_kernel(q_ref, k_ref, la_ref, v_ref, r_ref, gn_ref, s0_ref,
                     mo_ref, s_ref, b_scr):
    chunk = q_ref.shape[1]
    nsub = chunk // GLA_SUB

    @pl.when(pl.program_id(2) == 0)
    def _():
        s_ref[...] = s0_ref[...]

    q = q_ref[0]
    k = k_ref[0]
    v = v_ref[0]
    v_bf = v.astype(BF16)
    ri = lax.broadcasted_iota(jnp.int32, (chunk, chunk), 0)
    ci = lax.broadcasted_iota(jnp.int32, (chunk, chunk), 1)
    tri = (ri >= ci).astype(F32)
    b = jnp.dot(tri, la_ref[0], precision=HIGHEST, preferred_element_type=F32)
    b_scr[...] = b
    b_last = b[chunk - 1:chunk, :]

    s_old = s_ref[0, 0]
    o_inter = _dot((q * jnp.exp(b)).astype(BF16), s_old.astype(BF16))

    k_dec_t = (k * jnp.exp(b_last - b)).T
    decay_col = jnp.broadcast_to(jnp.exp(b_last), (GLA_DK, GLA_DK)).T
    decay = jnp.concatenate([decay_col] * (GLA_DV // GLA_DK), axis=1)
    s_ref[0, 0] = decay * s_old + _dot(k_dec_t.astype(BF16), v_bf)

    sub_row = lax.broadcasted_iota(jnp.int32, (GLA_SUB, GLA_DK), 0)
    gn = gn_ref[...]
    for blk in range(nsub):
        r0 = blk * GLA_SUB
        rows = slice(r0, r0 + GLA_SUB)
        q_i = q[rows]
        b_i = b[rows]
        acc = o_inter[rows]
        if blk > 0:
            ref_b = b_scr[r0 - 1:r0, :]
            q_t = (q_i * jnp.exp(b_i - ref_b)).astype(BF16)
            k_t = (k[:r0] * jnp.exp(ref_b - b[:r0])).astype(BF16)
            a_off = _dot_nt(q_t, k_t)
            acc = acc + _dot(a_off.astype(BF16), v_bf[:r0])
        for j in range(GLA_SUB):
            k_j = k_ref[0, r0 + j:r0 + j + 1, :]
            b_j = b_scr[r0 + j:r0 + j + 1, :]
            v_j = v_ref[0, r0 + j:r0 + j + 1, :]
            e = jnp.where(sub_row >= j, b_i - b_j, NEG_INF)
            w = jnp.sum(q_i * k_j * jnp.exp(e), axis=-1, keepdims=True)
            acc = acc + w * v_j
        o_n = _rms_norm(acc, gn) * _silu(r_ref[0, rows, :])
        mo_ref[0, rows, :] = o_n.astype(BF16)


def _gla_scan(q, k, la, v, r, g_norm, s0):
    bsz, t, _ = q.shape
    chunk = _row_tile(t, GLA_CHUNK)
    assert chunk % GLA_SUB == 0
    qk_spec = pl.BlockSpec((1, chunk, GLA_DK), lambda b, h, c: (b, c, h))
    v_spec = pl.BlockSpec((1, chunk, GLA_DV), lambda b, h, c: (b, c, h))
    s_spec = pl.BlockSpec((1, 1, GLA_DK, GLA_DV), lambda b, h, c: (b, h, 0, 0))
    return pl.pallas_call(
        _gla_scan_kernel,
        grid=(bsz, GLA_HEADS, t // chunk),
        in_specs=[qk_spec, qk_spec, qk_spec, v_spec, v_spec,
                  pl.BlockSpec((1, GLA_DV), lambda b, h, c: (0, 0)), s_spec],
        out_specs=[v_spec, s_spec],
        out_shape=[jax.ShapeDtypeStruct((bsz, t, GLA_V), BF16),
                   jax.ShapeDtypeStruct((bsz, GLA_HEADS, GLA_DK, GLA_DV), F32)],
        scratch_shapes=[pltpu.VMEM((chunk, GLA_DK), F32)],
        compiler_params=_cparams(("parallel", "parallel", "arbitrary"), 32),
        name="gla_scan",
    )(q, k, la, v, r, g_norm.reshape(1, GLA_DV), s0)


def _gla_step_kernel(q_ref, k_ref, la_ref, v_ref, r_ref, gn_ref, s0_ref, mo_ref, s_ref):
    gn = gn_ref[...]
    for h in range(GLA_HEADS):
        ks = slice(h * GLA_DK, (h + 1) * GLA_DK)
        vs = slice(h * GLA_DV, (h + 1) * GLA_DV)
        q = q_ref[0, :, ks]
        k = k_ref[0, :, ks]
        a = jnp.exp(la_ref[0, :, ks])
        v = v_ref[0, :, vs]
        s_old = s0_ref[0, h]

        def col(row_vec):
            sq = jnp.broadcast_to(row_vec, (GLA_DK, GLA_DK)).T
            return jnp.concatenate([sq] * (GLA_DV // GLA_DK), axis=1)

        s_ref[0, h] = col(a) * s_old + col(k) * v
        o = jnp.sum(col(q * a) * s_old, axis=0, keepdims=True) + jnp.sum(q * k, axis=-1, keepdims=True) * v
        o_n = _rms_norm(o, gn) * _silu(r_ref[0, :, vs])
        mo_ref[0, :, vs] = o_n.astype(BF16)


def _gla_step(q, k, la, v, r, g_norm, s0):
    n = q.shape[0]
    qk_spec = pl.BlockSpec((1, 1, GLA_QK), lambda b: (b, 0, 0))
    v_spec = pl.BlockSpec((1, 1, GLA_V), lambda b: (b, 0, 0))
    s_spec = pl.BlockSpec((1, GLA_HEADS, GLA_DK, GLA_DV), lambda b: (b, 0, 0, 0))
    r3 = lambda a: a.reshape(n, 1, a.shape[-1])
    mo, s_new = pl.pallas_call(
        _gla_step_kernel,
        grid=(n,),
        in_specs=[qk_spec, qk_spec, qk_spec, v_spec, v_spec,
                  pl.BlockSpec((1, GLA_DV), lambda b: (0, 0)), s_spec],
        out_specs=[v_spec, s_spec],
        out_shape=[jax.ShapeDtypeStruct((n, 1, GLA_V), BF16),
                   jax.ShapeDtypeStruct(s0.shape, F32)],
        compiler_params=_cparams(("parallel",), 16),
        name="gla_step",
    )(r3(q), r3(k), r3(la), r3(v), r3(r), g_norm.reshape(1, GLA_DV), s0)
    return mo.reshape(n, GLA_V), s_new


def _post_kernel(x_ref, mo_ref, wo_ref, g_ref, wup_ref, wdn_ref, y_ref, h_scr):
    @pl.when(pl.program_id(1) == 0)
    def _():
        x1 = x_ref[...] + _dot(mo_ref[...], wo_ref[...])
        y_ref[...] = x1
        h_scr[...] = _rms_norm(x1, g_ref[...]).astype(BF16)

    a = jnp.maximum(_dot(h_scr[...], wup_ref[...]), 0.0)
    y_ref[...] += _dot((a * a).astype(BF16), wdn_ref[...])


def _post(x, mo, w_out, g, w_up, w_down, tm_pref=512, tf_pref=2048):
    n, d = x.shape
    d_ff = w_up.shape[1]
    tm = _row_tile(n, tm_pref)
    tf = _row_tile(d_ff, tf_pref)
    return pl.pallas_call(
        _post_kernel,
        grid=(n // tm, d_ff // tf),
        in_specs=[pl.BlockSpec((tm, d), lambda i, j: (i, 0)),
                  pl.BlockSpec((tm, d), lambda i, j: (i, 0)),
                  pl.BlockSpec((d, d), lambda i, j: (0, 0)),
                  pl.BlockSpec((1, d), lambda i, j: (0, 0)),
                  pl.BlockSpec((d, tf), lambda i, j: (0, j)),
                  pl.BlockSpec((tf, d), lambda i, j: (j, 0))],
        out_specs=pl.BlockSpec((tm, d), lambda i, j: (i, 0)),
        out_shape=jax.ShapeDtypeStruct((n, d), F32),
        scratch_shapes=[pltpu.VMEM((tm, d), BF16)],
        compiler_params=_cparams(("parallel", "arbitrary"), 48),
        name="post_mlp",
    )(x, mo, w_out.astype(BF16), g.reshape(1, d), w_up.astype(BF16), w_down.astype(BF16))


def _seg_norm(x, g, seg):
    sq = x * x
    if seg == HEAD_W:
        scale = lax.rsqrt(jnp.mean(sq, axis=-1, keepdims=True) + EPS)
    else:
        assert 2 * seg == HEAD_W
        lo = lax.broadcasted_iota(jnp.int32, x.shape, 1) < seg
        s_lo = jnp.sum(jnp.where(lo, sq, 0.0), axis=-1, keepdims=True)
        s_hi = jnp.sum(jnp.where(lo, 0.0, sq), axis=-1, keepdims=True)
        scale = lax.rsqrt(jnp.where(lo, s_lo, s_hi) * (1.0 / seg) + EPS)
    return x * scale * g


def _attn_pre_kernel(x_ref, g_ref, win_ref, qn_ref, kn_ref, *out_refs, seg, q_scale, tb, emit_qf, emit_vt):
    out_refs = list(out_refs)
    qb_ref, k_ref, v_ref = out_refs[:3]
    qf_ref = out_refs[3] if emit_qf else None
    kb_ref, vt_ref = out_refs[-2:] if emit_vt else (None, None)
    d = x_ref.shape[1]
    h = _rms_norm(x_ref[...], g_ref[...]).astype(BF16)
    y = _dot(h, win_ref[...])
    tm = y.shape[0]
    for hd in range(ATTN_HEADS):
        sl = slice(hd * HEAD_W, (hd + 1) * HEAD_W)
        qh = _seg_norm(y[:, hd * HEAD_W:(hd + 1) * HEAD_W], qn_ref[:, sl], seg)
        kh = _seg_norm(y[:, d + hd * HEAD_W:d + (hd + 1) * HEAD_W], kn_ref[:, sl], seg)
        vh = y[:, 2 * d + hd * HEAD_W:2 * d + (hd + 1) * HEAD_W]
        qb_ref[:, sl] = (qh * q_scale).astype(BF16)
        k_ref[:, sl] = kh
        v_ref[:, sl] = vh
        if emit_qf:
            qf_ref[:, sl] = qh
        if emit_vt:
            kb_ref[:, sl] = kh.astype(BF16)
            for j in range(tm // tb):
                vt_ref[0, hd, j] = vh[j * tb:(j + 1) * tb].T.astype(BF16)


def _attn_pre(x, g, w_in, q_norm, k_norm, *, seg, q_scale, emit_qf=False, batch=None, tb=None, tm_pref=512):
    n, d = x.shape
    emit_vt = batch is not None
    tm = _row_tile(n, tm_pref)
    row = lambda i: (i, 0)
    full = lambda i: (0, 0)
    reps = d // q_norm.shape[0]
    dtypes = [BF16, F32, F32] + ([F32] if emit_qf else []) + ([BF16] if emit_vt else [])
    out_specs = [pl.BlockSpec((tm, d), row) for _ in dtypes]
    out_shape = [jax.ShapeDtypeStruct((n, d), dt) for dt in dtypes]
    if emit_vt:
        t = n // batch
        assert tm % tb == 0 and t % tm == 0
        tiles = t // tm
        out_specs.append(pl.BlockSpec((1, ATTN_HEADS, tm // tb, HEAD_W, tb),
                                      lambda i: (i // tiles, 0, i % tiles, 0, 0)))
        out_shape.append(jax.ShapeDtypeStruct((batch, ATTN_HEADS, t // tb, HEAD_W, tb), BF16))
    return pl.pallas_call(
        functools.partial(_attn_pre_kernel, seg=seg, q_scale=q_scale, tb=tb, emit_qf=emit_qf, emit_vt=emit_vt),
        grid=(n // tm,),
        in_specs=[pl.BlockSpec((tm, d), row), pl.BlockSpec((1, d), full),
                  pl.BlockSpec(w_in.shape, full), pl.BlockSpec((1, d), full), pl.BlockSpec((1, d), full)],
        out_specs=out_specs,
        out_shape=out_shape,
        compiler_params=_cparams(("parallel",), 56),
        name="attn_pre",
    )(x, g.reshape(1, d), w_in.astype(BF16),
      jnp.tile(q_norm, reps).reshape(1, d), jnp.tile(k_norm, reps).reshape(1, d))


def _bias_tile_kernel(bucket_ref, rb_ref, o_ref):
    h = pl.program_id(0)
    bucket = bucket_ref[...]
    far = rb_ref[FAR_BUCKET, h]
    acc = jnp.full(bucket.shape, NEG_INF, F32)
    for bk in range(REL_BUCKETS):
        acc = jnp.where(bucket == bk, (rb_ref[bk, h] - far) * LOG2E, acc)
    o_ref[0] = acc


def _bias_tiles(rel_bias, buckets):
    return pl.pallas_call(
        _bias_tile_kernel,
        grid=(ATTN_HEADS,),
        in_specs=[pl.BlockSpec(buckets.shape, lambda h: (0, 0, 0)),
                  pl.BlockSpec(memory_space=pltpu.SMEM)],
        out_specs=pl.BlockSpec((1,) + buckets.shape, lambda h: (h, 0, 0, 0)),
        out_shape=jax.ShapeDtypeStruct((ATTN_HEADS,) + buckets.shape, F32),
        compiler_params=_cparams(("parallel",), 32),
        name="bias_tiles",
    )(jnp.asarray(buckets), rel_bias)


def _block_buckets(tb):
    j = np.arange(tb)[:, None]
    i = np.arange(tb)[None, :]
    diag = np.where(i >= j, _rel_bucket_np(i - j), -1)
    prev = _rel_bucket_np(tb + i - j)
    return np.stack([diag, prev]).astype(np.int32)


def _softmax_group(s_list, vt_list, m_ref, l_ref, acc_ref):
    m_prev = m_ref[...]
    m_new = m_prev
    for s in s_list:
        m_new = jnp.maximum(m_new, jnp.max(s, axis=0, keepdims=True))
    alpha = jnp.exp2(m_prev - m_new)
    l_new = alpha * l_ref[...]
    acc = alpha * acc_ref[...]
    for s, vt in zip(s_list, vt_list):
        p = jnp.exp2(s - m_new)
        l_new = l_new + jnp.sum(p, axis=0, keepdims=True)
        acc = acc + _dot(vt, p.astype(BF16))
    m_ref[...] = m_new
    l_ref[...] = l_new
    acc_ref[...] = acc


def _diff_attn_kernel(q_ref, k_ref, vt_ref, bias_ref, lq1_ref, lk1_ref, lq2_ref, lk2_ref, subln_ref,
                      o_ref, m1, l1, a1, m2, l2, a2):
    tb = q_ref.shape[1]
    qi = pl.program_id(2)
    q = q_ref[0]
    lane = lax.broadcasted_iota(jnp.int32, q.shape, 1)
    q1 = jnp.where(lane < DIFF_DK, q, jnp.zeros_like(q))
    q2 = jnp.where(lane < DIFF_DK, jnp.zeros_like(q), q)

    for m_ref, l_ref, a_ref in ((m1, l1, a1), (m2, l2, a2)):
        m_ref[...] = jnp.full(m_ref.shape, NEG_INF, F32)
        l_ref[...] = jnp.zeros(l_ref.shape, F32)
        a_ref[...] = jnp.zeros(a_ref.shape, F32)

    def scores(kb):
        kblk = k_ref[0, pl.ds(pl.multiple_of(kb * tb, tb), tb), :]
        return _dot_nt(kblk, q1), _dot_nt(kblk, q2)

    def update(blocks, s_pairs):
        vts = [vt_ref[0, 0, kb] for kb in blocks]
        _softmax_group([s[0] for s in s_pairs], vts, m1, l1, a1)
        _softmax_group([s[1] for s in s_pairs], vts, m2, l2, a2)

    prev = jnp.maximum(qi - 1, 0)
    d1, d2 = scores(qi)
    p1, p2 = scores(prev)
    prev_bias = jnp.where(qi > 0, bias_ref[0, 1], NEG_INF)
    update([qi, prev], [(d1 + bias_ref[0, 0], d2 + bias_ref[0, 0]), (p1 + prev_bias, p2 + prev_bias)])

    n_far = jnp.maximum(qi - 1, 0)

    def far_group(g, carry):
        blocks = [g * DIFF_FAR_GROUP + i for i in range(DIFF_FAR_GROUP)]
        update(blocks, [scores(kb) for kb in blocks])
        return carry

    lax.fori_loop(0, n_far // DIFF_FAR_GROUP, far_group, 0)
    for i in range(DIFF_FAR_GROUP - 1):
        kb = (n_far // DIFF_FAR_GROUP) * DIFF_FAR_GROUP + i

        @pl.when(kb < n_far)
        def _(kb=kb):
            update([kb], [scores(kb)])

    lam = _lambda_full(lq1_ref, lk1_ref, lq2_ref, lk2_ref)
    o_t = a1[...] * (1.0 / l1[...]) - lam * (a2[...] * (1.0 / l2[...]))
    o = _rms_norm(o_t.T, subln_ref[...]) * (1.0 - DIFF_LAMBDA_INIT)
    o_ref[0] = o.astype(BF16)


def _diff_attn(qb, kb, vt, bias, lq1, lk1, lq2, lk2, subln):
    bsz, t, d = qb.shape
    tb = vt.shape[-1]
    nq = t // tb
    vec = lambda a: a.reshape(1, -1)
    vspec = lambda w: pl.BlockSpec((1, w), lambda b, h, i: (0, 0))
    stat = pltpu.VMEM((1, tb), F32)
    acc = pltpu.VMEM((HEAD_W, tb), F32)
    return pl.pallas_call(
        _diff_attn_kernel,
        grid=(bsz, ATTN_HEADS, nq),
        in_specs=[pl.BlockSpec((1, tb, HEAD_W), lambda b, h, i: (b, i, h)),
                  pl.BlockSpec((1, t, HEAD_W), lambda b, h, i: (b, 0, h)),
                  pl.BlockSpec((1, 1, nq, HEAD_W, tb), lambda b, h, i: (b, h, 0, 0, 0)),
                  pl.BlockSpec((1, 2, tb, tb), lambda b, h, i: (h, 0, 0, 0)),
                  vspec(DIFF_DK), vspec(DIFF_DK), vspec(DIFF_DK), vspec(DIFF_DK), vspec(HEAD_W)],
        out_specs=pl.BlockSpec((1, tb, HEAD_W), lambda b, h, i: (b, i, h)),
        out_shape=jax.ShapeDtypeStruct((bsz, t, d), BF16),
        scratch_shapes=[stat, stat, acc, stat, stat, acc],
        compiler_params=_cparams(("parallel", "parallel", "arbitrary"), 48),
        name="diff_attn",
    )(qb, kb, vt, bias, vec(lq1), vec(lk1), vec(lq2), vec(lk2), vec(subln))


def _diff_decode_kernel(pt_ref, q_ref, ks_ref, vs_ref, blast_ref, bself_ref,
                        lq1_ref, lk1_ref, lq2_ref, lk2_ref, subln_ref, *rest):
    g = DEC_PAGES_PER_STEP
    k_pages = rest[:g]
    v_pages = rest[g:2 * g]
    o_ref, m1, l1, a1, m2, l2, a2 = rest[2 * g:]
    step = pl.program_id(1)
    last = pl.num_programs(1) - 1

    q = q_ref[0]
    lane = lax.broadcasted_iota(jnp.int32, q.shape, 1)
    q_maps = (jnp.where(lane < DIFF_DK, q, 0.0), jnp.where(lane < DIFF_DK, 0.0, q))
    states = ((m1, l1, a1), (m2, l2, a2))

    @pl.when(step == 0)
    def _():
        for qm, (m_ref, l_ref, a_ref) in zip(q_maps, states):
            m_ref[...] = jnp.sum(qm * ks_ref[0], axis=-1, keepdims=True) + bself_ref[...]
            l_ref[...] = jnp.ones(l_ref.shape, F32)
            a_ref[...] = vs_ref[0]

    for i in range(g):
        k3 = k_pages[i][0]
        v3 = v_pages[i][0]
        for qm, (m_ref, l_ref, a_ref) in zip(q_maps, states):
            s = jnp.sum(k3 * qm, axis=-1, keepdims=True)
            if i == g - 1:
                s = s + jnp.where(step == last, blast_ref[...], 0.0)
            m_prev = m_ref[...]
            m_new = jnp.maximum(m_prev, jnp.max(s, axis=0))
            alpha = jnp.exp2(m_prev - m_new)
            p = jnp.exp2(s - m_new)
            l_ref[...] = alpha * l_ref[...] + jnp.sum(p, axis=0)
            a_ref[...] = alpha * a_ref[...] + jnp.sum(p * v3, axis=0)
            m_ref[...] = m_new

    @pl.when(step == last)
    def _():
        lam = _lambda_full(lq1_ref, lk1_ref, lq2_ref, lk2_ref)
        o = a1[...] * (1.0 / l1[...]) - lam * (a2[...] * (1.0 / l2[...]))
        o_ref[0] = (_rms_norm(o, subln_ref[...]) * (1.0 - DIFF_LAMBDA_INIT)).astype(BF16)


def _decode_bias(rel_bias):
    far = rel_bias[FAR_BUCKET]
    newest = (rel_bias[_rel_bucket_np(PAGE_SIZE - np.arange(PAGE_SIZE))] - far) * LOG2E
    own = (rel_bias[0] - far) * LOG2E
    return newest[:, :, None], own[:, None]


def _diff_decode(page_table, q, k_self, v_self, cache_k, cache_v, rel_bias, lq1, lk1, lq2, lk2, subln):
    n = q.shape[0]
    n_pages = page_table.shape[1]
    g = DEC_PAGES_PER_STEP
    assert n_pages % g == 0
    b_last, b_self = _decode_bias(rel_bias)
    per_head = lambda a: a.reshape(n, ATTN_HEADS, HEAD_W)
    vec = lambda a: a.reshape(1, -1)
    const = lambda shape: pl.BlockSpec(shape, lambda b, j, pt: (0,) * len(shape))
    seq = pl.BlockSpec((1, ATTN_HEADS, HEAD_W), lambda b, j, pt: (b, 0, 0))

    def page_spec(i):
        return pl.BlockSpec((1, PAGE_SIZE, ATTN_HEADS, HEAD_W),
                            lambda b, j, pt: (pt[b * n_pages + j * g + i], 0, 0, 0))

    stat = pltpu.VMEM((ATTN_HEADS, 1), F32)
    acc = pltpu.VMEM((ATTN_HEADS, HEAD_W), F32)
    out = pl.pallas_call(
        _diff_decode_kernel,
        grid_spec=pltpu.PrefetchScalarGridSpec(
            num_scalar_prefetch=1,
            grid=(n, n_pages // g),
            in_specs=[seq, seq, seq, const((PAGE_SIZE, ATTN_HEADS, 1)), const((ATTN_HEADS, 1)),
                      const((1, DIFF_DK)), const((1, DIFF_DK)), const((1, DIFF_DK)), const((1, DIFF_DK)),
                      const((1, HEAD_W))]
                     + [page_spec(i) for i in range(g)] + [page_spec(i) for i in range(g)],
            out_specs=seq,
            scratch_shapes=[stat, stat, acc, stat, stat, acc]),
        out_shape=jax.ShapeDtypeStruct((n, ATTN_HEADS, HEAD_W), BF16),
        compiler_params=_cparams(("parallel", "arbitrary"), 32),
        name="diff_decode",
    )(page_table.reshape(-1), per_head(q), per_head(k_self), per_head(v_self), b_last, b_self,
      vec(lq1), vec(lk1), vec(lq2), vec(lk2), vec(subln),
      *([cache_k] * g), *([cache_v] * g))
    return out.reshape(n, ATTN_HEADS * HEAD_W)


def _block_mean_kernel(k_ref, o_ref):
    for i in range(o_ref.shape[0]):
        o_ref[i:i + 1, :] = jnp.mean(k_ref[i * MOBA_BLOCK:(i + 1) * MOBA_BLOCK, :], axis=0, keepdims=True)


def _block_mean(k):
    n, d = k.shape
    nb = n // MOBA_BLOCK
    per = _row_tile(nb, 8)
    return pl.pallas_call(
        _block_mean_kernel,
        grid=(nb // per,),
        in_specs=[pl.BlockSpec((per * MOBA_BLOCK, d), lambda i: (i, 0))],
        out_specs=pl.BlockSpec((per, d), lambda i: (i, 0)),
        out_shape=jax.ShapeDtypeStruct((nb, d), F32),
        compiler_params=_cparams(("parallel",), 32),
        name="moba_block_mean",
    )(k)


def _topk_rank(gate, blk_idx, nb):
    rank = jnp.zeros(gate.shape, F32)
    for m in range(nb):
        gm = gate[m:m + 1, :]
        beats = (gm > gate) | ((gm == gate) & (blk_idx > m))
        rank = rank + beats.astype(F32)
    return rank


def _moba_attn_kernel(qf_ref, qb_ref, k_ref, vt_ref, kmean_ref, bias_ref, o_ref, sel_scr, m_scr, l_scr, acc_scr):
    tb = qf_ref.shape[1]
    nb = kmean_ref.shape[1]
    qi = pl.program_id(2)
    q = qb_ref[0]

    gate = _dot_nt(kmean_ref[0], qf_ref[0], precision=HIGHEST)
    blk_idx = lax.broadcasted_iota(jnp.int32, gate.shape, 0)
    past = blk_idx < qi
    gate = jnp.where(past, gate, NEG_INF)
    rank = _topk_rank(gate, blk_idx, nb)
    sel_scr[...] = (past & (rank < MOBA_TOPK)).astype(F32)

    m_scr[...] = jnp.full(m_scr.shape, NEG_INF, F32)
    l_scr[...] = jnp.zeros(l_scr.shape, F32)
    acc_scr[...] = jnp.zeros(acc_scr.shape, F32)

    nq = pl.num_programs(2)

    def scores(kb, bias, keep):
        kblk = k_ref[0, pl.ds(pl.multiple_of(kb * tb, tb), tb), :]
        s = _dot_nt(kblk, q)
        if bias is not None:
            s = s + bias
        if keep is not None:
            s = jnp.where(keep > 0.0, s, NEG_INF)
        return s

    prev = jnp.maximum(qi - 1, 0)
    _softmax_group([scores(qi, bias_ref[0, 0], None), scores(prev, bias_ref[0, 1], sel_scr[pl.ds(prev, 1), :])],
                   [vt_ref[0, 0, qi], vt_ref[0, 0, prev]], m_scr, l_scr, acc_scr)

    n_far = jnp.maximum(qi - 1, 0)

    def far_group(g, carry):
        s_list, vt_list = [], []
        for i in range(MOBA_FAR_GROUP):
            kb = g * MOBA_FAR_GROUP + i
            kb_safe = jnp.minimum(kb, nq - 1)
            keep = jnp.where(kb < n_far, sel_scr[pl.ds(kb_safe, 1), :], 0.0)
            s_list.append(scores(kb_safe, None, keep))
            vt_list.append(vt_ref[0, 0, kb_safe])
        _softmax_group(s_list, vt_list, m_scr, l_scr, acc_scr)
        return carry

    lax.fori_loop(0, (n_far + MOBA_FAR_GROUP - 1) // MOBA_FAR_GROUP, far_group, 0)
    o_ref[0] = (acc_scr[...] * (1.0 / l_scr[...])).T.astype(BF16)


def _moba_attn(qf, qb, kb, vt, kmean, bias):
    bsz, t, d = qb.shape
    tb = vt.shape[-1]
    assert tb == MOBA_BLOCK
    nq = t // tb
    nb = kmean.shape[1]
    qspec = pl.BlockSpec((1, tb, HEAD_W), lambda b, h, i: (b, i, h))
    return pl.pallas_call(
        _moba_attn_kernel,
        grid=(bsz, ATTN_HEADS, nq),
        in_specs=[qspec, qspec,
                  pl.BlockSpec((1, t, HEAD_W), lambda b, h, i: (b, 0, h)),
                  pl.BlockSpec((1, 1, nq, HEAD_W, tb), lambda b, h, i: (b, h, 0, 0, 0)),
                  pl.BlockSpec((1, nb, HEAD_W), lambda b, h, i: (b, 0, h)),
                  pl.BlockSpec((1, 2, tb, tb), lambda b, h, i: (h, 0, 0, 0))],
        out_specs=qspec,
        out_shape=jax.ShapeDtypeStruct((bsz, t, d), BF16),
        scratch_shapes=[pltpu.VMEM((nb, tb), F32), pltpu.VMEM((1, tb), F32), pltpu.VMEM((1, tb), F32),
                        pltpu.VMEM((HEAD_W, tb), F32)],
        compiler_params=_cparams(("parallel", "parallel", "arbitrary"), 32),
        name="moba_attn",
    )(qf, qb, kb, vt, kmean, bias)


def _moba_route_kernel(pt_ref, qf_ref, *rest):
    g = DEC_PAGES_PER_STEP
    k_pages = rest[:g]
    sel_ref, kmean_scr = rest[g:]
    step = pl.program_id(1)
    pages_per_block = MOBA_BLOCK // PAGE_SIZE
    blocks_per_step = g // pages_per_block
    nb = kmean_scr.shape[0]
    for i in range(blocks_per_step):
        tot = jnp.sum(k_pages[i * pages_per_block][0], axis=0)
        for p in range(1, pages_per_block):
            tot = tot + jnp.sum(k_pages[i * pages_per_block + p][0], axis=0)
        kmean_scr[step * blocks_per_step + i] = tot * (1.0 / MOBA_BLOCK)

    @pl.when(step == pl.num_programs(1) - 1)
    def _():
        gate = jnp.sum(kmean_scr[...] * qf_ref[0], axis=-1, keepdims=True)
        blk_idx = lax.broadcasted_iota(jnp.int32, gate.shape, 0)
        rank = _topk_rank(gate, blk_idx, nb)
        for t in range(MOBA_TOPK):
            chosen = jnp.sum(jnp.where(rank == t, blk_idx, 0), axis=0)
            sel_ref[0, t] = jnp.broadcast_to(chosen, sel_ref.shape[2:])


def _moba_route(page_table, qf, cache_k):
    n = qf.shape[0]
    n_pages = page_table.shape[1]
    g = DEC_PAGES_PER_STEP
    nb = n_pages * PAGE_SIZE // MOBA_BLOCK
    assert n_pages % g == 0 and g % (MOBA_BLOCK // PAGE_SIZE) == 0 and nb >= MOBA_TOPK

    def page_spec(i):
        return pl.BlockSpec((1, PAGE_SIZE, ATTN_HEADS, HEAD_W),
                            lambda b, j, pt: (pt[b * n_pages + j * g + i], 0, 0, 0))

    return pl.pallas_call(
        _moba_route_kernel,
        grid_spec=pltpu.PrefetchScalarGridSpec(
            num_scalar_prefetch=1,
            grid=(n, n_pages // g),
            in_specs=[pl.BlockSpec((1, ATTN_HEADS, HEAD_W), lambda b, j, pt: (b, 0, 0))]
                     + [page_spec(i) for i in range(g)],
            out_specs=pl.BlockSpec((1, MOBA_TOPK, ATTN_HEADS, HEAD_W), lambda b, j, pt: (b, 0, 0, 0)),
            scratch_shapes=[pltpu.VMEM((nb, ATTN_HEADS, HEAD_W), F32)]),
        out_shape=jax.ShapeDtypeStruct((n, MOBA_TOPK, ATTN_HEADS, HEAD_W), jnp.int32),
        compiler_params=_cparams(("parallel", "arbitrary"), 32),
        name="moba_route",
    )(page_table.reshape(-1), qf.reshape(n, ATTN_HEADS, HEAD_W), *([cache_k] * g))


def _moba_decode_kernel(pt_ref, sel_ref, q_ref, ks_ref, vs_ref, bias_ref, ck_ref, cv_ref, o_ref,
                        k_buf, v_buf, sems, *, n_pages, newest_block):
    pages_per_block = MOBA_BLOCK // PAGE_SIZE
    n_sel = MOBA_TOPK * pages_per_block
    b = pl.program_id(0)

    def block_of(h, t):
        return sel_ref[(b * MOBA_TOPK + t) * ATTN_HEADS + h]

    def copies(h, i):
        page = pt_ref[b * n_pages + block_of(h, i // pages_per_block) * pages_per_block + i % pages_per_block]
        return (pltpu.make_async_copy(ck_ref.at[page, :, h, :], k_buf.at[h, i], sems.at[0, h, i]),
                pltpu.make_async_copy(cv_ref.at[page, :, h, :], v_buf.at[h, i], sems.at[1, h, i]))

    for h in range(ATTN_HEADS):
        for i in range(n_sel):
            for cp in copies(h, i):
                cp.start()

    for h in range(ATTN_HEADS):
        q = q_ref[0, h:h + 1, :]
        q8 = jnp.broadcast_to(q, (8, HEAD_W)).astype(BF16)
        s_self = jnp.sum(q * ks_ref[0, h:h + 1, :], axis=-1, keepdims=True) + bias_ref[h, 1:2, 0:1]
        scores = []
        m = s_self
        for i in range(n_sel):
            for cp in copies(h, i):
                cp.wait()
            s = _dot_nt(q8, k_buf[h, i].astype(BF16))[0:1]
            if i % pages_per_block == pages_per_block - 1:
                s = s + jnp.where(block_of(h, i // pages_per_block) == newest_block, bias_ref[h, 0:1, :], 0.0)
            scores.append(s)
            m = jnp.maximum(m, jnp.max(s, axis=-1, keepdims=True))
        p_self = jnp.exp2(s_self - m)
        l = p_self
        o = p_self * vs_ref[0, h:h + 1, :]
        for i, s in enumerate(scores):
            p = jnp.exp2(s - m)
            l = l + jnp.sum(p, axis=-1, keepdims=True)
            p8 = jnp.broadcast_to(p, (8, PAGE_SIZE)).astype(BF16)
            o = o + _dot(p8, v_buf[h, i].astype(BF16))[0:1]
        o_ref[0, h:h + 1, :] = (o * (1.0 / l)).astype(BF16)


def _moba_decode(page_table, sel, q, k_self, v_self, cache_k, cache_v, rel_bias):
    n = q.shape[0]
    n_pages = page_table.shape[1]
    pages_per_block = MOBA_BLOCK // PAGE_SIZE
    n_sel = MOBA_TOPK * pages_per_block
    b_last, b_self = _decode_bias(rel_bias)
    bias = jnp.stack([b_last[:, :, 0].T, jnp.broadcast_to(b_self, (ATTN_HEADS, PAGE_SIZE))], axis=1)
    per_head = lambda a: a.reshape(n, ATTN_HEADS, HEAD_W)
    seq = pl.BlockSpec((1, ATTN_HEADS, HEAD_W), lambda b, pt, sl: (b, 0, 0))
    page_buf = pltpu.VMEM((ATTN_HEADS, n_sel, PAGE_SIZE, HEAD_W), F32)
    out = pl.pallas_call(
        functools.partial(_moba_decode_kernel, n_pages=n_pages, newest_block=n_pages // pages_per_block - 1),
        grid_spec=pltpu.PrefetchScalarGridSpec(
            num_scalar_prefetch=2,
            grid=(n,),
            in_specs=[seq, seq, seq, pl.BlockSpec(bias.shape, lambda b, pt, sl: (0, 0, 0)),
                      pl.BlockSpec(memory_space=pl.ANY), pl.BlockSpec(memory_space=pl.ANY)],
            out_specs=seq,
            scratch_shapes=[page_buf, page_buf, pltpu.SemaphoreType.DMA((2, ATTN_HEADS, n_sel))]),
        out_shape=jax.ShapeDtypeStruct((n, ATTN_HEADS, HEAD_W), BF16),
        compiler_params=_cparams(("arbitrary",), 32),
        name="moba_decode",
    )(page_table.reshape(-1), sel[:, :, :, 0].reshape(-1),
      per_head(q), per_head(k_self), per_head(v_self), bias, cache_k, cache_v)
    return out.reshape(n, ATTN_HEADS * HEAD_W)


def kernel(x_prompt, x_sample, state_gla_l0, cache_k_l1, cache_v_l1, cache_k_l2, cache_v_l2, state_gla_l3, page_table, rel_bias, norm_mix, norm_mlp, mlp_w_up, mlp_w_down, gla_w_in, gla_w_a1, gla_w_a2, gla_b_a, gla_norm, gla_w_out, diff_w_in, diff_q_norm, diff_k_norm, diff_lq1, diff_lk1, diff_lq2, diff_lk2, diff_subln, diff_w_out, moba_w_in, moba_q_norm, moba_k_norm, moba_w_out):
    bsz, t, d = x_prompt.shape
    n_dec = x_sample.shape[0]
    assert d == D_MODEL and x_sample.shape[1] == 1
    xp = x_prompt.reshape(bsz * t, d)
    xs = x_sample.reshape(n_dec, d)
    seq = lambda a: a.reshape(bsz, t, a.shape[-1])
    heads = lambda a: a.reshape(a.shape[:-1] + (ATTN_HEADS, HEAD_W))

    def mlp(x, mo, w_out, li):
        return _post(x, mo, w_out, norm_mlp[li], mlp_w_up[li], mlp_w_down[li])

    def gla_layer(xp, xs, li, gi, state_dec):
        w = (gla_w_in[gi], gla_w_a1[gi], gla_w_a2[gi], gla_b_a[gi])
        q, k, v, r, la = _gla_pre(xp, norm_mix[li], *w)
        s0 = jnp.zeros((bsz,) + state_dec.shape[1:], F32)
        mo_p, st_p = _gla_scan(seq(q), seq(k), seq(la), seq(v), seq(r), gla_norm[gi], s0)
        q, k, v, r, la = _gla_pre(xs, norm_mix[li], *w)
        mo_s, st_s = _gla_step(q, k, la, v, r, gla_norm[gi], state_dec)
        xp = mlp(xp, mo_p.reshape(bsz * t, d), gla_w_out[gi], li)
        xs = mlp(xs, mo_s, gla_w_out[gi], li)
        return xp, xs, st_p, st_s

    xp, xs, st_p0, st_s0 = gla_layer(xp, xs, 0, 0, state_gla_l0)

    tb = min(DIFF_TB, t)
    lam_w = (diff_lq1, diff_lk1, diff_lq2, diff_lk2, diff_subln)
    pre = dict(seg=DIFF_DK, q_scale=DIFF_DK ** -0.5 * LOG2E)
    qb, k, v, kb, vt = _attn_pre(xp, norm_mix[1], diff_w_in, diff_q_norm, diff_k_norm, batch=bsz, tb=tb, **pre)
    bias = _bias_tiles(rel_bias, _block_buckets(tb))
    mo_p = _diff_attn(seq(qb), seq(kb), vt, bias, *lam_w)
    k_p1, v_p1 = heads(seq(k)), heads(seq(v))
    qb, k, v = _attn_pre(xs, norm_mix[1], diff_w_in, diff_q_norm, diff_k_norm, **pre)
    mo_s = _diff_decode(page_table, qb.astype(F32), k, v, cache_k_l1, cache_v_l1, rel_bias, *lam_w)
    k_s1, v_s1 = heads(k.reshape(n_dec, 1, d)), heads(v.reshape(n_dec, 1, d))
    xp = mlp(xp, mo_p.reshape(bsz * t, d), diff_w_out, 1)
    xs = mlp(xs, mo_s, diff_w_out, 1)

    pre = dict(seg=HEAD_W, q_scale=HEAD_W ** -0.5 * LOG2E, emit_qf=True)
    qb, k, v, qf, kb, vt = _attn_pre(xp, norm_mix[2], moba_w_in, moba_q_norm, moba_k_norm, batch=bsz, tb=MOBA_TB, **pre)
    kmean = _block_mean(k).reshape(bsz, t // MOBA_BLOCK, d)
    bias = _bias_tiles(rel_bias, _block_buckets(MOBA_TB))
    mo_p = _moba_attn(seq(qf), seq(qb), seq(kb), vt, kmean, bias)
    k_p2, v_p2 = heads(seq(k)), heads(seq(v))
    qb, k, v, qf = _attn_pre(xs, norm_mix[2], moba_w_in, moba_q_norm, moba_k_norm, **pre)
    sel = _moba_route(page_table, qf, cache_k_l2)
    mo_s = _moba_decode(page_table, sel, qb.astype(F32), k, v, cache_k_l2, cache_v_l2, rel_bias)
    k_s2, v_s2 = heads(k.reshape(n_dec, 1, d)), heads(v.reshape(n_dec, 1, d))
    xp = mlp(xp, mo_p.reshape(bsz * t, d), moba_w_out, 2)
    xs = mlp(xs, mo_s, moba_w_out, 2)

    xp, xs, st_p3, st_s3 = gla_layer(xp, xs, 3, 1, state_gla_l3)

    return (xp.reshape(bsz, t, d), xs.reshape(n_dec, 1, d), st_p0, st_s0,
            k_p1, v_p1, k_s1, v_s1, k_p2, v_p2, k_s2, v_s2, st_p3, st_s3)
```

```python
import functools
import math

import numpy as np
import jax
import jax.numpy as jnp
from jax import lax
from jax.experimental import pallas as pl
from jax.experimental.pallas import tpu as pltpu

F32 = jnp.float32
BF16 = jnp.bfloat16
NEG_INF = float("-inf")
HIGHEST = lax.Precision.HIGHEST

EPS = 1e-6
D_MODEL = 1024
PAGE_SIZE = 128

GLA_HEADS = 4
GLA_DK = 128
GLA_DV = 256
GLA_QK = GLA_HEADS * GLA_DK
GLA_V = GLA_HEADS * GLA_DV
GLA_TAU = 16.0
GLA_RANK_PAD = 128
GLA_CHUNK = 128
GLA_SUB = 16

ATTN_HEADS = 8
HEAD_W = 128
DIFF_DK = 64
DIFF_LAMBDA_INIT = 0.8 - 0.6 * math.exp(-0.3 * 1)
MOBA_BLOCK = 256
MOBA_TOPK = 3
DIFF_TB = 512
MOBA_TB = MOBA_BLOCK
DIFF_FAR_GROUP = 2
MOBA_FAR_GROUP = 4
LOG2E = math.log2(math.e)

REL_BUCKETS = 32
REL_MAX_EXACT = 16
REL_MAX_DIST = 128
FAR_BUCKET = REL_BUCKETS - 1

V7X_VMEM_BYTES = 64 * 1024 * 1024
MIB = 1024 * 1024
DEC_PAGES_PER_STEP = 8

NT_DIMS = (((1,), (1,)), ((), ()))


def _cparams(sem, vmem_mib):
    assert vmem_mib * MIB < V7X_VMEM_BYTES
    return pltpu.CompilerParams(dimension_semantics=sem, vmem_limit_bytes=vmem_mib * MIB)


def _rel_bucket_np(dist):
    dist = np.maximum(dist, 0)
    df = np.maximum(dist, 1).astype(np.float32)
    large = REL_MAX_EXACT + (np.log(df / np.float32(REL_MAX_EXACT)) / np.float32(math.log(REL_MAX_DIST / REL_MAX_EXACT))
                             * np.float32(REL_BUCKETS - REL_MAX_EXACT)).astype(np.int32)
    large = np.minimum(large, REL_BUCKETS - 1)
    return np.where(dist < REL_MAX_EXACT, dist, large).astype(np.int32)


def _rms_norm(x, g):
    return x * lax.rsqrt(jnp.mean(x * x, axis=-1, keepdims=True) + EPS) * g


def _log_sigmoid(x):
    return jnp.minimum(x, 0.0) - jnp.log1p(jnp.exp(-jnp.abs(x)))


def _silu(x):
    return x * (1.0 / (1.0 + jnp.exp(-x)))


def _dot(a, b):
    return jnp.dot(a, b, preferred_element_type=F32)


def _dot_nt(a, b, precision=None):
    return lax.dot_general(a, b, NT_DIMS, precision=precision, preferred_element_type=F32)


def _lambda_full(lq1_ref, lk1_ref, lq2_ref, lk2_ref):
    return (jnp.exp(jnp.sum(lq1_ref[...] * lk1_ref[...], axis=-1, keepdims=True))
            - jnp.exp(jnp.sum(lq2_ref[...] * lk2_ref[...], axis=-1, keepdims=True))
            + DIFF_LAMBDA_INIT)


def _row_tile(n, pref):
    t = min(n, pref)
    assert n % t == 0
    return t


def _gla_pre_kernel(x_ref, g_ref, win_ref, wa1_ref, wa2_ref, ba_ref,
                    q_ref, k_ref, v_ref, r_ref, la_ref):
    h = _rms_norm(x_ref[...], g_ref[...]).astype(BF16)
    y = _dot(h, win_ref[...])
    q_ref[...] = y[:, :GLA_QK] * (GLA_DK ** -0.5)
    k_ref[...] = y[:, GLA_QK:2 * GLA_QK]
    v_ref[...] = y[:, 2 * GLA_QK:2 * GLA_QK + GLA_V]
    r_ref[...] = y[:, 2 * GLA_QK + GLA_V:]
    low = _dot(h, wa1_ref[...]).astype(BF16)
    gate = _dot(low, wa2_ref[...]) + ba_ref[...]
    la_ref[...] = _log_sigmoid(gate) * (1.0 / GLA_TAU)


def _gla_pre(x, g, w_in, w_a1, w_a2, b_a, tm_pref=512):
    n, d = x.shape
    tm = _row_tile(n, tm_pref)
    rank = w_a1.shape[1]
    wa1 = jnp.pad(w_a1, ((0, 0), (0, GLA_RANK_PAD - rank))).astype(BF16)
    wa2 = jnp.pad(w_a2, ((0, GLA_RANK_PAD - rank), (0, 0))).astype(BF16)
    row = lambda i: (i, 0)
    full = lambda i: (0, 0)
    widths = (GLA_QK, GLA_QK, GLA_V, GLA_V, GLA_QK)
    return pl.pallas_call(
        _gla_pre_kernel,
        grid=(n // tm,),
        in_specs=[pl.BlockSpec((tm, d), row), pl.BlockSpec((1, d), full),
                  pl.BlockSpec(w_in.shape, full), pl.BlockSpec(wa1.shape, full),
                  pl.BlockSpec(wa2.shape, full), pl.BlockSpec((1, GLA_QK), full)],
        out_specs=[pl.BlockSpec((tm, w), row) for w in widths],
        out_shape=[jax.ShapeDtypeStruct((n, w), F32) for w in widths],
        compiler_params=_cparams(("parallel",), 48),
        name="gla_pre",
    )(x, g.reshape(1, d), w_in.astype(BF16), wa1, wa2, b_a.reshape(1, GLA_QK))


def _gla_scan_kernel(q_ref, k_ref, la_ref, v_ref, r_ref, gn_ref, s0_ref,
                     mo_ref, s_ref, b_scr):
    chunk = q_ref.shape[1]
    nsub = chunk // GLA_SUB

    @pl.when(pl.program_id(2) == 0)
    def _():
        s_ref[...] = s0_ref[...]

    q = q_ref[0]
    k = k_ref[0]
    v = v_ref[0]
    v_bf = v.astype(BF16)
    ri = lax.broadcasted_iota(jnp.int32, (chunk, chunk), 0)
    ci = lax.broadcasted_iota(jnp.int32, (chunk, chunk), 1)
    tri = (ri >= ci).astype(F32)
    b = jnp.dot(tri, la_ref[0], precision=HIGHEST, preferred_element_type=F32)
    b_scr[...] = b
    b_last = b[chunk - 1:chunk, :]

    s_old = s_ref[0, 0]
    o_inter = _dot((q * jnp.exp(b)).astype(BF16), s_old.astype(BF16))

    k_dec_t = (k * jnp.exp(b_last - b)).T
    decay_col = jnp.broadcast_to(jnp.exp(b_last), (GLA_DK, GLA_DK)).T
    decay = jnp.concatenate([decay_col] * (GLA_DV // GLA_DK), axis=1)
    s_ref[0, 0] = decay * s_old + _dot(k_dec_t.astype(BF16), v_bf)

    sub_row = lax.broadcasted_iota(jnp.int32, (GLA_SUB, GLA_DK), 0)
    gn = gn_ref[...]
    for blk in range(nsub):
        r0 = blk * GLA_SUB
        rows = slice(r0, r0 + GLA_SUB)
        q_i = q[rows]
        b_i = b[rows]
        acc = o_inter[rows]
        if blk > 0:
            ref_b = b_scr[r0 - 1:r0, :]
            q_t = (q_i * jnp.exp(b_i - ref_b)).astype(BF16)
            k_t = (k[:r0] * jnp.exp(ref_b - b[:r0])).astype(BF16)
            a_off = _dot_nt(q_t, k_t)
            acc = acc + _dot(a_off.astype(BF16), v_bf[:r0])
        for j in range(GLA_SUB):
            k_j = k_ref[0, r0 + j:r0 + j + 1, :]
            b_j = b_scr[r0 + j:r0 + j + 1, :]
            v_j = v_ref[0, r0 + j:r0 + j + 1, :]
            e = jnp.where(sub_row >= j, b_i - b_j, NEG_INF)
            w = jnp.sum(q_i * k_j * jnp.exp(e), axis=-1, keepdims=True)
            acc = acc + w * v_j
        o_n = _rms_norm(acc, gn) * _silu(r_ref[0, rows, :])
        mo_ref[0, rows, :] = o_n.astype(BF16)


def _gla_scan(q, k, la, v, r, g_norm, s0):
    bsz, t, _ = q.shape
    chunk = _row_tile(t, GLA_CHUNK)
    assert chunk % GLA_SUB == 0
    qk_spec = pl.BlockSpec((1, chunk, GLA_DK), lambda b, h, c: (b, c, h))
    v_spec = pl.BlockSpec((1, chunk, GLA_DV), lambda b, h, c: (b, c, h))
    s_spec = pl.BlockSpec((1, 1, GLA_DK, GLA_DV), lambda b, h, c: (b, h, 0, 0))
    return pl.pallas_call(
        _gla_scan_kernel,
        grid=(bsz, GLA_HEADS, t // chunk),
        in_specs=[qk_spec, qk_spec, qk_spec, v_spec, v_spec,
                  pl.BlockSpec((1, GLA_DV), lambda b, h, c: (0, 0)), s_spec],
        out_specs=[v_spec, s_spec],
        out_shape=[jax.ShapeDtypeStruct((bsz, t, GLA_V), BF16),
                   jax.ShapeDtypeStruct((bsz, GLA_HEADS, GLA_DK, GLA_DV), F32)],
        scratch_shapes=[pltpu.VMEM((chunk, GLA_DK), F32)],
        compiler_params=_cparams(("parallel", "parallel", "arbitrary"), 32),
        name="gla_scan",
    )(q, k, la, v, r, g_norm.reshape(1, GLA_DV), s0)


def _gla_step_kernel(q_ref, k_ref, la_ref, v_ref, r_ref, gn_ref, s0_ref, mo_ref, s_ref):
    gn = gn_ref[...]
    for h in range(GLA_HEADS):
        ks = slice(h * GLA_DK, (h + 1) * GLA_DK)
        vs = slice(h * GLA_DV, (h + 1) * GLA_DV)
        q = q_ref[0, :, ks]
        k = k_ref[0, :, ks]
        a = jnp.exp(la_ref[0, :, ks])
        v = v_ref[0, :, vs]
        s_old = s0_ref[0, h]

        def col(row_vec):
            sq = jnp.broadcast_to(row_vec, (GLA_DK, GLA_DK)).T
            return jnp.concatenate([sq] * (GLA_DV // GLA_DK), axis=1)

        s_ref[0, h] = col(a) * s_old + col(k) * v
        o = jnp.sum(col(q * a) * s_old, axis=0, keepdims=True) + jnp.sum(q * k, axis=-1, keepdims=True) * v
        o_n = _rms_norm(o, gn) * _silu(r_ref[0, :, vs])
        mo_ref[0, :, vs] = o_n.astype(BF16)


def _gla_step(q, k, la, v, r, g_norm, s0):
    n = q.shape[0]
    qk_spec = pl.BlockSpec((1, 1, GLA_QK), lambda b: (b, 0, 0))
    v_spec = pl.BlockSpec((1, 1, GLA_V), lambda b: (b, 0, 0))
    s_spec = pl.BlockSpec((1, GLA_HEADS, GLA_DK, GLA_DV), lambda b: (b, 0, 0, 0))
    r3 = lambda a: a.reshape(n, 1, a.shape[-1])
    mo, s_new = pl.pallas_call(
        _gla_step_kernel,
        grid=(n,),
        in_specs=[qk_spec, qk_spec, qk_spec, v_spec, v_spec,
                  pl.BlockSpec((1, GLA_DV), lambda b: (0, 0)), s_spec],
        out_specs=[v_spec, s_spec],
        out_shape=[jax.ShapeDtypeStruct((n, 1, GLA_V), BF16),
                   jax.ShapeDtypeStruct(s0.shape, F32)],
        compiler_params=_cparams(("parallel",), 16),
        name="gla_step",
    )(r3(q), r3(k), r3(la), r3(v), r3(r), g_norm.reshape(1, GLA_DV), s0)
    return mo.reshape(n, GLA_V), s_new


def _post_kernel(x_ref, mo_ref, wo_ref, g_ref, wup_ref, wdn_ref, y_ref, h_scr):
    @pl.when(pl.program_id(1) == 0)
    def _():
        x1 = x_ref[...] + _dot(mo_ref[...], wo_ref[...])
        y_ref[...] = x1
        h_scr[...] = _rms_norm(x1, g_ref[...]).astype(BF16)

    a = jnp.maximum(_dot(h_scr[...], wup_ref[...]), 0.0)
    y_ref[...] += _dot((a * a).astype(BF16), wdn_ref[...])


def _post(x, mo, w_out, g, w_up, w_down, tm_pref=512, tf_pref=2048):
    n, d = x.shape
    d_ff = w_up.shape[1]
    tm = _row_tile(n, tm_pref)
    tf = _row_tile(d_ff, tf_pref)
    return pl.pallas_call(
        _post_kernel,
        grid=(n // tm, d_ff // tf),
        in_specs=[pl.BlockSpec((tm, d), lambda i, j: (i, 0)),
                  pl.BlockSpec((tm, d), lambda i, j: (i, 0)),
                  pl.BlockSpec((d, d), lambda i, j: (0, 0)),
                  pl.BlockSpec((1, d), lambda i, j: (0, 0)),
                  pl.BlockSpec((d, tf), lambda i, j: (0, j)),
                  pl.BlockSpec((tf, d), lambda i, j: (j, 0))],
        out_specs=pl.BlockSpec((tm, d), lambda i, j: (i, 0)),
        out_shape=jax.ShapeDtypeStruct((n, d), F32),
        scratch_shapes=[pltpu.VMEM((tm, d), BF16)],
        compiler_params=_cparams(("parallel", "arbitrary"), 48),
        name="post_mlp",
    )(x, mo, w_out.astype(BF16), g.reshape(1, d), w_up.astype(BF16), w_down.astype(BF16))


def _seg_norm(x, g, seg):
    sq = x * x
    if seg == HEAD_W:
        scale = lax.rsqrt(jnp.mean(sq, axis=-1, keepdims=True) + EPS)
    else:
        assert 2 * seg == HEAD_W
        lo = lax.broadcasted_iota(jnp.int32, x.shape, 1) < seg
        s_lo = jnp.sum(jnp.where(lo, sq, 0.0), axis=-1, keepdims=True)
        s_hi = jnp.sum(jnp.where(lo, 0.0, sq), axis=-1, keepdims=True)
        scale = lax.rsqrt(jnp.where(lo, s_lo, s_hi) * (1.0 / seg) + EPS)
    return x * scale * g


def _attn_pre_kernel(x_ref, g_ref, win_ref, qn_ref, kn_ref, *out_refs, seg, q_scale, tb, emit_qf, emit_vt):
    out_refs = list(out_refs)
    qb_ref, k_ref, v_ref = out_refs[:3]
    qf_ref = out_refs[3] if emit_qf else None
    kb_ref, vt_ref = out_refs[-2:] if emit_vt else (None, None)
    d = x_ref.shape[1]
    h = _rms_norm(x_ref[...], g_ref[...]).astype(BF16)
    y = _dot(h, win_ref[...])
    tm = y.shape[0]
    for hd in range(ATTN_HEADS):
        sl = slice(hd * HEAD_W, (hd + 1) * HEAD_W)
        qh = _seg_norm(y[:, hd * HEAD_W:(hd + 1) * HEAD_W], qn_ref[:, sl], seg)
        kh = _seg_norm(y[:, d + hd * HEAD_W:d + (hd + 1) * HEAD_W], kn_ref[:, sl], seg)
        vh = y[:, 2 * d + hd * HEAD_W:2 * d + (hd + 1) * HEAD_W]
        qb_ref[:, sl] = (qh * q_scale).astype(BF16)
        k_ref[:, sl] = kh
        v_ref[:, sl] = vh
        if emit_qf:
            qf_ref[:, sl] = qh
        if emit_vt:
            kb_ref[:, sl] = kh.astype(BF16)
            for j in range(tm // tb):
                vt_ref[0, hd, j] = vh[j * tb:(j + 1) * tb].T.astype(BF16)


def _attn_pre(x, g, w_in, q_norm, k_norm, *, seg, q_scale, emit_qf=False, batch=None, tb=None, tm_pref=512):
    n, d = x.shape
    emit_vt = batch is not None
    tm = _row_tile(n, tm_pref)
    row = lambda i: (i, 0)
    full = lambda i: (0, 0)
    reps = d // q_norm.shape[0]
    dtypes = [BF16, F32, F32] + ([F32] if emit_qf else []) + ([BF16] if emit_vt else [])
    out_specs = [pl.BlockSpec((tm, d), row) for _ in dtypes]
    out_shape = [jax.ShapeDtypeStruct((n, d), dt) for dt in dtypes]
    if emit_vt:
        t = n // batch
        assert tm % tb == 0 and t % tm == 0
        tiles = t // tm
        out_specs.append(pl.BlockSpec((1, ATTN_HEADS, tm // tb, HEAD_W, tb),
                                      lambda i: (i // tiles, 0, i % tiles, 0, 0)))
        out_shape.append(jax.ShapeDtypeStruct((batch, ATTN_HEADS, t // tb, HEAD_W, tb), BF16))
    return pl.pallas_call(
        functools.partial(_attn_pre_kernel, seg=seg, q_scale=q_scale, tb=tb, emit_qf=emit_qf, emit_vt=emit_vt),
        grid=(n // tm,),
        in_specs=[pl.BlockSpec((tm, d), row), pl.BlockSpec((1, d), full),
                  pl.BlockSpec(w_in.shape, full), pl.BlockSpec((1, d), full), pl.BlockSpec((1, d), full)],
        out_specs=out_specs,
        out_shape=out_shape,
        compiler_params=_cparams(("parallel",), 56),
        name="attn_pre",
    )(x, g.reshape(1, d), w_in.astype(BF16),
      jnp.tile(q_norm, reps).reshape(1, d), jnp.tile(k_norm, reps).reshape(1, d))


def _bias_tile_kernel(bucket_ref, rb_ref, o_ref):
    h = pl.program_id(0)
    bucket = bucket_ref[...]
    far = rb_ref[FAR_BUCKET, h]
    acc = jnp.full(bucket.shape, NEG_INF, F32)
    for bk in range(REL_BUCKETS):
        acc = jnp.where(bucket == bk, (rb_ref[bk, h] - far) * LOG2E, acc)
    o_ref[0] = acc


def _bias_tiles(rel_bias, buckets):
    return pl.pallas_call(
        _bias_tile_kernel,
        grid=(ATTN_HEADS,),
        in_specs=[pl.BlockSpec(buckets.shape, lambda h: (0, 0, 0)),
                  pl.BlockSpec(memory_space=pltpu.SMEM)],
        out_specs=pl.BlockSpec((1,) + buckets.shape, lambda h: (h, 0, 0, 0)),
        out_shape=jax.ShapeDtypeStruct((ATTN_HEADS,) + buckets.shape, F32),
        compiler_params=_cparams(("parallel",), 32),
        name="bias_tiles",
    )(jnp.asarray(buckets), rel_bias)


def _block_buckets(tb):
    j = np.arange(tb)[:, None]
    i = np.arange(tb)[None, :]
    diag = np.where(i >= j, _rel_bucket_np(i - j), -1)
    prev = _rel_bucket_np(tb + i - j)
    return np.stack([diag, prev]).astype(np.int32)


def _softmax_group(s_list, vt_list, m_ref, l_ref, acc_ref):
    m_prev = m_ref[...]
    m_new = m_prev
    for s in s_list:
        m_new = jnp.maximum(m_new, jnp.max(s, axis=0, keepdims=True))
    alpha = jnp.exp2(m_prev - m_new)
    l_new = alpha * l_ref[...]
    acc = alpha * acc_ref[...]
    for s, vt in zip(s_list, vt_list):
        p = jnp.exp2(s - m_new)
        l_new = l_new + jnp.sum(p, axis=0, keepdims=True)
        acc = acc + _dot(vt, p.astype(BF16))
    m_ref[...] = m_new
    l_ref[...] = l_new
    acc_ref[...] = acc


def _drain_far_groups(n_groups, qk, softmax, buf_a, buf_b):
    def pair(j, carry):
        g = 2 * j
        qk(g + 1, buf_b)
        softmax(g, buf_a)
        qk(g + 2, buf_a)
        softmax(g + 1, buf_b)
        return carry

    lax.fori_loop(0, jnp.maximum(n_groups - 1, 0) // 2, pair, 0)
    last = n_groups - 1
    odd = jnp.bitwise_and(n_groups, 1) == 1

    @pl.when(odd)
    def _():
        softmax(last, buf_a)

    @pl.when(jnp.logical_and(n_groups > 0, jnp.logical_not(odd)))
    def _():
        qk(last, buf_b)
        softmax(last - 1, buf_a)
        softmax(last, buf_b)


def _diff_attn_kernel(q_ref, k_ref, vt_ref, bias_ref, lq1_ref, lk1_ref, lq2_ref, lk2_ref, subln_ref,
                      o_ref, s_a, s_b, m1, l1, a1, m2, l2, a2):
    tb = q_ref.shape[1]
    qi = pl.program_id(2)
    nq = pl.num_programs(2)
    q = q_ref[0]
    lane = lax.broadcasted_iota(jnp.int32, q.shape, 1)
    q1 = jnp.where(lane < DIFF_DK, q, jnp.zeros_like(q))
    q2 = jnp.where(lane < DIFF_DK, jnp.zeros_like(q), q)

    for m_ref, l_ref, a_ref in ((m1, l1, a1), (m2, l2, a2)):
        m_ref[...] = jnp.full(m_ref.shape, NEG_INF, F32)
        l_ref[...] = jnp.zeros(l_ref.shape, F32)
        a_ref[...] = jnp.zeros(a_ref.shape, F32)

    def scores(kb):
        kblk = k_ref[0, pl.ds(pl.multiple_of(kb * tb, tb), tb), :]
        return _dot_nt(kblk, q1), _dot_nt(kblk, q2)

    def update(blocks, s_pairs):
        vts = [vt_ref[0, 0, kb] for kb in blocks]
        _softmax_group([s[0] for s in s_pairs], vts, m1, l1, a1)
        _softmax_group([s[1] for s in s_pairs], vts, m2, l2, a2)

    n_far = jnp.maximum(qi - 1, 0)
    n_groups = n_far // DIFF_FAR_GROUP

    def far_blocks(g):
        return [jnp.minimum(g * DIFF_FAR_GROUP + i, nq - 1) for i in range(DIFF_FAR_GROUP)]

    def qk_far_group(g, buf):
        for i, kb in enumerate(far_blocks(g)):
            buf[i, 0], buf[i, 1] = scores(kb)

    def softmax_far_group(g, buf):
        update(far_blocks(g), [(buf[i, 0], buf[i, 1]) for i in range(DIFF_FAR_GROUP)])

    qk_far_group(0, s_a)

    prev = jnp.maximum(qi - 1, 0)
    d1, d2 = scores(qi)
    p1, p2 = scores(prev)
    prev_bias = jnp.where(qi > 0, bias_ref[0, 1], NEG_INF)
    update([qi, prev], [(d1 + bias_ref[0, 0], d2 + bias_ref[0, 0]), (p1 + prev_bias, p2 + prev_bias)])

    _drain_far_groups(n_groups, qk_far_group, softmax_far_group, s_a, s_b)

    for i in range(DIFF_FAR_GROUP - 1):
        kb = n_groups * DIFF_FAR_GROUP + i

        @pl.when(kb < n_far)
        def _(kb=kb):
            update([kb], [scores(kb)])

    lam = _lambda_full(lq1_ref, lk1_ref, lq2_ref, lk2_ref)
    o_t = a1[...] * (1.0 / l1[...]) - lam * (a2[...] * (1.0 / l2[...]))
    o = _rms_norm(o_t.T, subln_ref[...]) * (1.0 - DIFF_LAMBDA_INIT)
    o_ref[0] = o.astype(BF16)


def _diff_attn(qb, kb, vt, bias, lq1, lk1, lq2, lk2, subln):
    bsz, t, d = qb.shape
    tb = vt.shape[-1]
    nq = t // tb
    vec = lambda a: a.reshape(1, -1)
    vspec = lambda w: pl.BlockSpec((1, w), lambda b, h, i: (0, 0))
    stat = pltpu.VMEM((1, tb), F32)
    acc = pltpu.VMEM((HEAD_W, tb), F32)
    s_buf = pltpu.VMEM((DIFF_FAR_GROUP, 2, tb, tb), F32)
    return pl.pallas_call(
        _diff_attn_kernel,
        grid=(bsz, ATTN_HEADS, nq),
        in_specs=[pl.BlockSpec((1, tb, HEAD_W), lambda b, h, i: (b, i, h)),
                  pl.BlockSpec((1, t, HEAD_W), lambda b, h, i: (b, 0, h)),
                  pl.BlockSpec((1, 1, nq, HEAD_W, tb), lambda b, h, i: (b, h, 0, 0, 0)),
                  pl.BlockSpec((1, 2, tb, tb), lambda b, h, i: (h, 0, 0, 0)),
                  vspec(DIFF_DK), vspec(DIFF_DK), vspec(DIFF_DK), vspec(DIFF_DK), vspec(HEAD_W)],
        out_specs=pl.BlockSpec((1, tb, HEAD_W), lambda b, h, i: (b, i, h)),
        out_shape=jax.ShapeDtypeStruct((bsz, t, d), BF16),
        scratch_shapes=[s_buf, s_buf, stat, stat, acc, stat, stat, acc],
        compiler_params=_cparams(("parallel", "parallel", "arbitrary"), 56),
        name="diff_attn",
    )(qb, kb, vt, bias, vec(lq1), vec(lk1), vec(lq2), vec(lk2), vec(subln))


def _diff_decode_kernel(pt_ref, q_ref, ks_ref, vs_ref, blast_ref, bself_ref,
                        lq1_ref, lk1_ref, lq2_ref, lk2_ref, subln_ref, *rest):
    g = DEC_PAGES_PER_STEP
    k_pages = rest[:g]
    v_pages = rest[g:2 * g]
    o_ref, m1, l1, a1, m2, l2, a2 = rest[2 * g:]
    step = pl.program_id(1)
    last = pl.num_programs(1) - 1

    q = q_ref[0]
    lane = lax.broadcasted_iota(jnp.int32, q.shape, 1)
    q_maps = (jnp.where(lane < DIFF_DK, q, 0.0), jnp.where(lane < DIFF_DK, 0.0, q))
    states = ((m1, l1, a1), (m2, l2, a2))
    rr = lax.broadcasted_iota(jnp.int32, (2 * HEAD_W, 2 * HEAD_W), 0)
    cc = lax.broadcasted_iota(jnp.int32, (2 * HEAD_W, 2 * HEAD_W), 1)
    lane_sum = ((rr < HEAD_W) == (cc < HEAD_W)).astype(BF16)

    @pl.when(step == 0)
    def _():
        for qm, (m_ref, l_ref, a_ref) in zip(q_maps, states):
            s_self = jnp.sum(qm * ks_ref[0], axis=-1, keepdims=True) + bself_ref[...]
            m_ref[...] = jnp.broadcast_to(s_self, m_ref.shape)
            l_ref[...] = jnp.ones(l_ref.shape, F32)
            a_ref[...] = vs_ref[0]

    for i in range(g):
        k3 = k_pages[i][0]
        v3 = v_pages[i][0]
        prod = jnp.concatenate([k3 * q_maps[0], k3 * q_maps[1]], axis=-1)
        sums = _dot(prod.reshape(PAGE_SIZE * ATTN_HEADS, 2 * HEAD_W).astype(BF16), lane_sum)
        sums = sums.reshape(PAGE_SIZE, ATTN_HEADS, 2 * HEAD_W)
        for mp, (m_ref, l_ref, a_ref) in enumerate(states):
            s = sums[:, :, mp * HEAD_W:(mp + 1) * HEAD_W]
            if i == g - 1:
                s = s + jnp.where(step == last, blast_ref[...], 0.0)
            m_prev = m_ref[...]
            m_new = jnp.maximum(m_prev, jnp.max(s, axis=0))
            alpha = jnp.exp2(m_prev - m_new)
            p = jnp.exp2(s - m_new)
            l_ref[...] = alpha * l_ref[...] + jnp.sum(p, axis=0)
            a_ref[...] = alpha * a_ref[...] + jnp.sum(p * v3, axis=0)
            m_ref[...] = m_new

    @pl.when(step == last)
    def _():
        lam = _lambda_full(lq1_ref, lk1_ref, lq2_ref, lk2_ref)
        o = a1[...] * (1.0 / l1[...]) - lam * (a2[...] * (1.0 / l2[...]))
        o_ref[0] = (_rms_norm(o, subln_ref[...]) * (1.0 - DIFF_LAMBDA_INIT)).astype(BF16)


def _decode_bias(rel_bias):
    far = rel_bias[FAR_BUCKET]
    newest = (rel_bias[_rel_bucket_np(PAGE_SIZE - np.arange(PAGE_SIZE))] - far) * LOG2E
    own = (rel_bias[0] - far) * LOG2E
    return newest[:, :, None], own[:, None]


def _diff_decode(page_table, q, k_self, v_self, cache_k, cache_v, rel_bias, lq1, lk1, lq2, lk2, subln):
    n = q.shape[0]
    n_pages = page_table.shape[1]
    g = DEC_PAGES_PER_STEP
    assert n_pages % g == 0
    b_last, b_self = _decode_bias(rel_bias)
    per_head = lambda a: a.reshape(n, ATTN_HEADS, HEAD_W)
    vec = lambda a: a.reshape(1, -1)
    const = lambda shape: pl.BlockSpec(shape, lambda b, j, pt: (0,) * len(shape))
    seq = pl.BlockSpec((1, ATTN_HEADS, HEAD_W), lambda b, j, pt: (b, 0, 0))

    def page_spec(i):
        return pl.BlockSpec((1, PAGE_SIZE, ATTN_HEADS, HEAD_W),
                            lambda b, j, pt: (pt[b * n_pages + j * g + i], 0, 0, 0))

    stat = pltpu.VMEM((ATTN_HEADS, HEAD_W), F32)
    acc = pltpu.VMEM((ATTN_HEADS, HEAD_W), F32)
    out = pl.pallas_call(
        _diff_decode_kernel,
        grid_spec=pltpu.PrefetchScalarGridSpec(
            num_scalar_prefetch=1,
            grid=(n, n_pages // g),
            in_specs=[seq, seq, seq, const((PAGE_SIZE, ATTN_HEADS, 1)), const((ATTN_HEADS, 1)),
                      const((1, DIFF_DK)), const((1, DIFF_DK)), const((1, DIFF_DK)), const((1, DIFF_DK)),
                      const((1, HEAD_W))]
                     + [page_spec(i) for i in range(g)] + [page_spec(i) for i in range(g)],
            out_specs=seq,
            scratch_shapes=[stat, stat, acc, stat, stat, acc]),
        out_shape=jax.ShapeDtypeStruct((n, ATTN_HEADS, HEAD_W), BF16),
        compiler_params=_cparams(("parallel", "arbitrary"), 48),
        name="diff_decode",
    )(page_table.reshape(-1), per_head(q), per_head(k_self), per_head(v_self), b_last, b_self,
      vec(lq1), vec(lk1), vec(lq2), vec(lk2), vec(subln),
      *([cache_k] * g), *([cache_v] * g))
    return out.reshape(n, ATTN_HEADS * HEAD_W)


def _block_mean_kernel(k_ref, o_ref):
    for i in range(o_ref.shape[0]):
        o_ref[i:i + 1, :] = jnp.mean(k_ref[i * MOBA_BLOCK:(i + 1) * MOBA_BLOCK, :], axis=0, keepdims=True)


def _block_mean(k):
    n, d = k.shape
    nb = n // MOBA_BLOCK
    per = _row_tile(nb, 8)
    return pl.pallas_call(
        _block_mean_kernel,
        grid=(nb // per,),
        in_specs=[pl.BlockSpec((per * MOBA_BLOCK, d), lambda i: (i, 0))],
        out_specs=pl.BlockSpec((per, d), lambda i: (i, 0)),
        out_shape=jax.ShapeDtypeStruct((nb, d), F32),
        compiler_params=_cparams(("parallel",), 32),
        name="moba_block_mean",
    )(k)


def _topk_rank(gate, blk_idx, nb):
    rank = jnp.zeros(gate.shape, F32)
    for m in range(nb):
        gm = gate[m:m + 1, :]
        beats = (gm > gate) | ((gm == gate) & (blk_idx > m))
        rank = rank + beats.astype(F32)
    return rank


def _moba_attn_kernel(qf_ref, qb_ref, k_ref, vt_ref, kmean_ref, bias_ref, o_ref,
                      sel_scr, s_a, s_b, m_scr, l_scr, acc_scr):
    tb = qf_ref.shape[1]
    nb = kmean_ref.shape[1]
    qi = pl.program_id(2)
    q = qb_ref[0]

    gate = _dot_nt(kmean_ref[0], qf_ref[0], precision=HIGHEST)
    blk_idx = lax.broadcasted_iota(jnp.int32, gate.shape, 0)
    past = blk_idx < qi
    gate = jnp.where(past, gate, NEG_INF)
    rank = _topk_rank(gate, blk_idx, nb)
    sel_scr[...] = (past & (rank < MOBA_TOPK)).astype(F32)

    m_scr[...] = jnp.full(m_scr.shape, NEG_INF, F32)
    l_scr[...] = jnp.zeros(l_scr.shape, F32)
    acc_scr[...] = jnp.zeros(acc_scr.shape, F32)

    nq = pl.num_programs(2)

    def scores(kb, bias, keep):
        kblk = k_ref[0, pl.ds(pl.multiple_of(kb * tb, tb), tb), :]
        s = _dot_nt(kblk, q)
        if bias is not None:
            s = s + bias
        if keep is not None:
            s = jnp.where(keep > 0.0, s, NEG_INF)
        return s

    n_far = jnp.maximum(qi - 1, 0)
    n_groups = (n_far + MOBA_FAR_GROUP - 1) // MOBA_FAR_GROUP

    def far_blocks(g):
        return [jnp.minimum(g * MOBA_FAR_GROUP + i, nq - 1) for i in range(MOBA_FAR_GROUP)]

    def qk_far_group(g, buf):
        for i, kb in enumerate(far_blocks(g)):
            keep = jnp.where(g * MOBA_FAR_GROUP + i < n_far, sel_scr[pl.ds(kb, 1), :], 0.0)
            buf[i] = scores(kb, None, keep)

    def softmax_far_group(g, buf):
        _softmax_group([buf[i] for i in range(MOBA_FAR_GROUP)], [vt_ref[0, 0, kb] for kb in far_blocks(g)],
                       m_scr, l_scr, acc_scr)

    qk_far_group(0, s_a)

    prev = jnp.maximum(qi - 1, 0)
    _softmax_group([scores(qi, bias_ref[0, 0], None), scores(prev, bias_ref[0, 1], sel_scr[pl.ds(prev, 1), :])],
                   [vt_ref[0, 0, qi], vt_ref[0, 0, prev]], m_scr, l_scr, acc_scr)

    _drain_far_groups(n_groups, qk_far_group, softmax_far_group, s_a, s_b)
    o_ref[0] = (acc_scr[...] * (1.0 / l_scr[...])).T.astype(BF16)


def _moba_attn(qf, qb, kb, vt, kmean, bias):
    bsz, t, d = qb.shape
    tb = vt.shape[-1]
    assert tb == MOBA_BLOCK
    nq = t // tb
    nb = kmean.shape[1]
    qspec = pl.BlockSpec((1, tb, HEAD_W), lambda b, h, i: (b, i, h))
    s_buf = pltpu.VMEM((MOBA_FAR_GROUP, tb, tb), F32)
    return pl.pallas_call(
        _moba_attn_kernel,
        grid=(bsz, ATTN_HEADS, nq),
        in_specs=[qspec, qspec,
                  pl.BlockSpec((1, t, HEAD_W), lambda b, h, i: (b, 0, h)),
                  pl.BlockSpec((1, 1, nq, HEAD_W, tb), lambda b, h, i: (b, h, 0, 0, 0)),
                  pl.BlockSpec((1, nb, HEAD_W), lambda b, h, i: (b, 0, h)),
                  pl.BlockSpec((1, 2, tb, tb), lambda b, h, i: (h, 0, 0, 0))],
        out_specs=qspec,
        out_shape=jax.ShapeDtypeStruct((bsz, t, d), BF16),
        scratch_shapes=[pltpu.VMEM((nb, tb), F32), s_buf, s_buf,
                        pltpu.VMEM((1, tb), F32), pltpu.VMEM((1, tb), F32),
                        pltpu.VMEM((HEAD_W, tb), F32)],
        compiler_params=_cparams(("parallel", "parallel", "arbitrary"), 32),
        name="moba_attn",
    )(qf, qb, kb, vt, kmean, bias)


def _moba_route_kernel(pt_ref, qf_ref, *rest):
    g = DEC_PAGES_PER_STEP
    k_pages = rest[:g]
    sel_ref, kmean_scr = rest[g:]
    step = pl.program_id(1)
    pages_per_block = MOBA_BLOCK // PAGE_SIZE
    blocks_per_step = g // pages_per_block
    nb = kmean_scr.shape[0]
    for i in range(blocks_per_step):
        tot = jnp.sum(k_pages[i * pages_per_block][0], axis=0)
        for p in range(1, pages_per_block):
            tot = tot + jnp.sum(k_pages[i * pages_per_block + p][0], axis=0)
        kmean_scr[step * blocks_per_step + i] = tot * (1.0 / MOBA_BLOCK)

    @pl.when(step == pl.num_programs(1) - 1)
    def _():
        gate = jnp.sum(kmean_scr[...] * qf_ref[0], axis=-1, keepdims=True)
        blk_idx = lax.broadcasted_iota(jnp.int32, gate.shape, 0)
        rank = _topk_rank(gate, blk_idx, nb)
        for t in range(MOBA_TOPK):
            chosen = jnp.sum(jnp.where(rank == t, blk_idx, 0), axis=0)
            sel_ref[0, t] = jnp.broadcast_to(chosen, sel_ref.shape[2:])


def _moba_route(page_table, qf, cache_k):
    n = qf.shape[0]
    n_pages = page_table.shape[1]
    g = DEC_PAGES_PER_STEP
    nb = n_pages * PAGE_SIZE // MOBA_BLOCK
    assert n_pages % g == 0 and g % (MOBA_BLOCK // PAGE_SIZE) == 0 and nb >= MOBA_TOPK

    def page_spec(i):
        return pl.BlockSpec((1, PAGE_SIZE, ATTN_HEADS, HEAD_W),
                            lambda b, j, pt: (pt[b * n_pages + j * g + i], 0, 0, 0))

    return pl.pallas_call(
        _moba_route_kernel,
        grid_spec=pltpu.PrefetchScalarGridSpec(
            num_scalar_prefetch=1,
            grid=(n, n_pages // g),
            in_specs=[pl.BlockSpec((1, ATTN_HEADS, HEAD_W), lambda b, j, pt: (b, 0, 0))]
                     + [page_spec(i) for i in range(g)],
            out_specs=pl.BlockSpec((1, MOBA_TOPK, ATTN_HEADS, HEAD_W), lambda b, j, pt: (b, 0, 0, 0)),
            scratch_shapes=[pltpu.VMEM((nb, ATTN_HEADS, HEAD_W), F32)]),
        out_shape=jax.ShapeDtypeStruct((n, MOBA_TOPK, ATTN_HEADS, HEAD_W), jnp.int32),
        compiler_params=_cparams(("parallel", "arbitrary"), 32),
        name="moba_route",
    )(page_table.reshape(-1), qf.reshape(n, ATTN_HEADS, HEAD_W), *([cache_k] * g))


def _moba_decode_kernel(pt_ref, sel_ref, q_ref, ks_ref, vs_ref, bias_ref, ck_ref, cv_ref, o_ref,
                        k_buf, v_buf, sems, *, n_pages, newest_block):
    pages_per_block = MOBA_BLOCK // PAGE_SIZE
    n_sel = MOBA_TOPK * pages_per_block
    b = pl.program_id(0)

    def block_of(h, t):
        return sel_ref[(b * MOBA_TOPK + t) * ATTN_HEADS + h]

    def copies(h, i):
        page = pt_ref[b * n_pages + block_of(h, i // pages_per_block) * pages_per_block + i % pages_per_block]
        return (pltpu.make_async_copy(ck_ref.at[page, :, h, :], k_buf.at[h, i], sems.at[0, h, i]),
                pltpu.make_async_copy(cv_ref.at[page, :, h, :], v_buf.at[h, i], sems.at[1, h, i]))

    for h in range(ATTN_HEADS):
        for i in range(n_sel):
            for cp in copies(h, i):
                cp.start()

    for h in range(ATTN_HEADS):
        for i in range(n_sel):
            for cp in copies(h, i):
                cp.wait()

    for h in range(ATTN_HEADS):
        q = q_ref[0, h:h + 1, :]
        q8 = jnp.broadcast_to(q, (8, HEAD_W)).astype(BF16)
        s_self = jnp.sum(q * ks_ref[0, h:h + 1, :], axis=-1, keepdims=True) + bias_ref[h, 1:2, 0:1]
        k_all = k_buf[h].reshape(n_sel * PAGE_SIZE, HEAD_W).astype(BF16)
        v_all = v_buf[h].reshape(n_sel * PAGE_SIZE, HEAD_W).astype(BF16)
        near = [jnp.where((i % pages_per_block == pages_per_block - 1) & (block_of(h, i // pages_per_block) == newest_block),
                          bias_ref[h, 0:1, :], 0.0) for i in range(n_sel)]
        s = _dot_nt(q8, k_all)[0:1] + jnp.concatenate(near, axis=-1)
        m = jnp.maximum(s_self, jnp.max(s, axis=-1, keepdims=True))
        p_self = jnp.exp2(s_self - m)
        p = jnp.exp2(s - m)
        l = p_self + jnp.sum(p, axis=-1, keepdims=True)
        o = p_self * vs_ref[0, h:h + 1, :] + _dot(jnp.broadcast_to(p, (8, p.shape[1])).astype(BF16), v_all)[0:1]
        o_ref[0, h:h + 1, :] = (o * (1.0 / l)).astype(BF16)


def _moba_decode(page_table, sel, q, k_self, v_self, cache_k, cache_v, rel_bias):
    n = q.shape[0]
    n_pages = page_table.shape[1]
    pages_per_block = MOBA_BLOCK // PAGE_SIZE
    n_sel = MOBA_TOPK * pages_per_block
    b_last, b_self = _decode_bias(rel_bias)
    bias = jnp.stack([b_last[:, :, 0].T, jnp.broadcast_to(b_self, (ATTN_HEADS, PAGE_SIZE))], axis=1)
    per_head = lambda a: a.reshape(n, ATTN_HEADS, HEAD_W)
    seq = pl.BlockSpec((1, ATTN_HEADS, HEAD_W), lambda b, pt, sl: (b, 0, 0))
    page_buf = pltpu.VMEM((ATTN_HEADS, n_sel, PAGE_SIZE, HEAD_W), F32)
    out = pl.pallas_call(
        functools.partial(_moba_decode_kernel, n_pages=n_pages, newest_block=n_pages // pages_per_block - 1),
        grid_spec=pltpu.PrefetchScalarGridSpec(
            num_scalar_prefetch=2,
            grid=(n,),
            in_specs=[seq, seq, seq, pl.BlockSpec(bias.shape, lambda b, pt, sl: (0, 0, 0)),
                      pl.BlockSpec(memory_space=pl.ANY), pl.BlockSpec(memory_space=pl.ANY)],
            out_specs=seq,
            scratch_shapes=[page_buf, page_buf, pltpu.SemaphoreType.DMA((2, ATTN_HEADS, n_sel))]),
        out_shape=jax.ShapeDtypeStruct((n, ATTN_HEADS, HEAD_W), BF16),
        compiler_params=_cparams(("arbitrary",), 32),
        name="moba_decode",
    )(page_table.reshape(-1), sel[:, :, :, 0].reshape(-1),
      per_head(q), per_head(k_self), per_head(v_self), bias, cache_k, cache_v)
    return out.reshape(n, ATTN_HEADS * HEAD_W)


def kernel(x_prompt, x_sample, state_gla_l0, cache_k_l1, cache_v_l1, cache_k_l2, cache_v_l2, state_gla_l3, page_table, rel_bias, norm_mix, norm_mlp, mlp_w_up, mlp_w_down, gla_w_in, gla_w_a1, gla_w_a2, gla_b_a, gla_norm, gla_w_out, diff_w_in, diff_q_norm, diff_k_norm, diff_lq1, diff_lk1, diff_lq2, diff_lk2, diff_subln, diff_w_out, moba_w_in, moba_q_norm, moba_k_norm, moba_w_out):
    bsz, t, d = x_prompt.shape
    n_dec = x_sample.shape[0]
    assert d == D_MODEL and x_sample.shape[1] == 1
    xp = x_prompt.reshape(bsz * t, d)
    xs = x_sample.reshape(n_dec, d)
    seq = lambda a: a.reshape(bsz, t, a.shape[-1])
    heads = lambda a: a.reshape(a.shape[:-1] + (ATTN_HEADS, HEAD_W))

    def mlp(x, mo, w_out, li):
        return _post(x, mo, w_out, norm_mlp[li], mlp_w_up[li], mlp_w_down[li])

    def gla_layer(xp, xs, li, gi, state_dec):
        w = (gla_w_in[gi], gla_w_a1[gi], gla_w_a2[gi], gla_b_a[gi])
        q, k, v, r, la = _gla_pre(xp, norm_mix[li], *w)
        s0 = jnp.zeros((bsz,) + state_dec.shape[1:], F32)
        mo_p, st_p = _gla_scan(seq(q), seq(k), seq(la), seq(v), seq(r), gla_norm[gi], s0)
        q, k, v, r, la = _gla_pre(xs, norm_mix[li], *w)
        mo_s, st_s = _gla_step(q, k, la, v, r, gla_norm[gi], state_dec)
        xp = mlp(xp, mo_p.reshape(bsz * t, d), gla_w_out[gi], li)
        xs = mlp(xs, mo_s, gla_w_out[gi], li)
        return xp, xs, st_p, st_s

    xp, xs, st_p0, st_s0 = gla_layer(xp, xs, 0, 0, state_gla_l0)

    tb = min(DIFF_TB, t)
    lam_w = (diff_lq1, diff_lk1, diff_lq2, diff_lk2, diff_subln)
    pre = dict(seg=DIFF_DK, q_scale=DIFF_DK ** -0.5 * LOG2E)
    qb, k, v, kb, vt = _attn_pre(xp, norm_mix[1], diff_w_in, diff_q_norm, diff_k_norm, batch=bsz, tb=tb, **pre)
    bias = _bias_tiles(rel_bias, _block_buckets(tb))
    mo_p = _diff_attn(seq(qb), seq(kb), vt, bias, *lam_w)
    k_p1, v_p1 = heads(seq(k)), heads(seq(v))
    qb, k, v = _attn_pre(xs, norm_mix[1], diff_w_in, diff_q_norm, diff_k_norm, **pre)
    mo_s = _diff_decode(page_table, qb.astype(F32), k, v, cache_k_l1, cache_v_l1, rel_bias, *lam_w)
    k_s1, v_s1 = heads(k.reshape(n_dec, 1, d)), heads(v.reshape(n_dec, 1, d))
    xp = mlp(xp, mo_p.reshape(bsz * t, d), diff_w_out, 1)
    xs = mlp(xs, mo_s, diff_w_out, 1)

    pre = dict(seg=HEAD_W, q_scale=HEAD_W ** -0.5 * LOG2E, emit_qf=True)
    qb, k, v, qf, kb, vt = _attn_pre(xp, norm_mix[2], moba_w_in, moba_q_norm, moba_k_norm, batch=bsz, tb=MOBA_TB, **pre)
    kmean = _block_mean(k).reshape(bsz, t // MOBA_BLOCK, d)
    bias = _bias_tiles(rel_bias, _block_buckets(MOBA_TB))
    mo_p = _moba_attn(seq(qf), seq(qb), seq(kb), vt, kmean, bias)
    k_p2, v_p2 = heads(seq(k)), heads(seq(v))
    qb, k, v, qf = _attn_pre(xs, norm_mix[2], moba_w_in, moba_q_norm, moba_k_norm, **pre)
    sel = _moba_route(page_table, qf, cache_k_l2)
    mo_s = _moba_decode(page_table, sel, qb.astype(F32), k, v, cache_k_l2, cache_v_l2, rel_bias)
    k_s2, v_s2 = heads(k.reshape(n_dec, 1, d)), heads(v.reshape(n_dec, 1, d))
    xp = mlp(xp, mo_p.reshape(bsz * t, d), moba_w_out, 2)
    xs = mlp(xs, mo_s, moba_w_out, 2)

    xp, xs, st_p3, st_s3 = gla_layer(xp, xs, 3, 1, state_gla_l3)

    return (xp.reshape(bsz, t, d), xs.reshape(n_dec, 1, d), st_p0, st_s0,
            k_p1, v_p1, k_s1, v_s1, k_p2, v_p2, k_s2, v_s2, st_p3, st_s3)
```

```python
import functools
import math

import numpy as np
import jax
import jax.numpy as jnp
from jax import lax
from jax.experimental import pallas as pl
from jax.experimental.pallas import tpu as pltpu

F32 = jnp.float32
BF16 = jnp.bfloat16
NEG_INF = float("-inf")
HIGHEST = lax.Precision.HIGHEST

EPS = 1e-6
D_MODEL = 1024
PAGE_SIZE = 128

GLA_HEADS = 4
GLA_DK = 128
GLA_DV = 256
GLA_QK = GLA_HEADS * GLA_DK
GLA_V = GLA_HEADS * GLA_DV
GLA_TAU = 16.0
GLA_RANK_PAD = 128
GLA_CHUNK = 128
GLA_SUB = 16

ATTN_HEADS = 8
HEAD_W = 128
DIFF_DK = 64
DIFF_LAMBDA_INIT = 0.8 - 0.6 * math.exp(-0.3 * 1)
MOBA_BLOCK = 256
MOBA_TOPK = 3
DIFF_TB = 512
MOBA_TB = MOBA_BLOCK
DIFF_FAR_GROUP = 2
MOBA_FAR_GROUP = 4
LOG2E = math.log2(math.e)

REL_BUCKETS = 32
REL_MAX_EXACT = 16
REL_MAX_DIST = 128
FAR_BUCKET = REL_BUCKETS - 1

V7X_VMEM_BYTES = 64 * 1024 * 1024
MIB = 1024 * 1024
DEC_PAGES_PER_STEP = 8

NT_DIMS = (((1,), (1,)), ((), ()))


def _cparams(sem, vmem_mib):
    assert vmem_mib * MIB < V7X_VMEM_BYTES
    return pltpu.CompilerParams(dimension_semantics=sem, vmem_limit_bytes=vmem_mib * MIB)


def _rel_bucket_np(dist):
    dist = np.maximum(dist, 0)
    df = np.maximum(dist, 1).astype(np.float32)
    large = REL_MAX_EXACT + (np.log(df / np.float32(REL_MAX_EXACT)) / np.float32(math.log(REL_MAX_DIST / REL_MAX_EXACT))
                             * np.float32(REL_BUCKETS - REL_MAX_EXACT)).astype(np.int32)
    large = np.minimum(large, REL_BUCKETS - 1)
    return np.where(dist < REL_MAX_EXACT, dist, large).astype(np.int32)


def _rms_norm(x, g):
    return x * lax.rsqrt(jnp.mean(x * x, axis=-1, keepdims=True) + EPS) * g


def _log_sigmoid(x):
    return jnp.minimum(x, 0.0) - jnp.log1p(jnp.exp(-jnp.abs(x)))


def _silu(x):
    return x * (1.0 / (1.0 + jnp.exp(-x)))


def _dot(a, b):
    return jnp.dot(a, b, preferred_element_type=F32)


def _dot_nt(a, b, precision=None):
    return lax.dot_general(a, b, NT_DIMS, precision=precision, preferred_element_type=F32)


def _lambda_full(lq1_ref, lk1_ref, lq2_ref, lk2_ref):
    return (jnp.exp(jnp.sum(lq1_ref[...] * lk1_ref[...], axis=-1, keepdims=True))
            - jnp.exp(jnp.sum(lq2_ref[...] * lk2_ref[...], axis=-1, keepdims=True))
            + DIFF_LAMBDA_INIT)


def _row_tile(n, pref):
    t = min(n, pref)
    assert n % t == 0
    return t


def _gla_pre_kernel(x_ref, g_ref, win_ref, wa1_ref, wa2_ref, ba_ref,
                    q_ref, k_ref, v_ref, r_ref, la_ref):
    h = _rms_norm(x_ref[...], g_ref[...]).astype(BF16)
    y = _dot(h, win_ref[...])
    q_ref[...] = y[:, :GLA_QK] * (GLA_DK ** -0.5)
    k_ref[...] = y[:, GLA_QK:2 * GLA_QK]
    v_ref[...] = y[:, 2 * GLA_QK:2 * GLA_QK + GLA_V]
    r_ref[...] = y[:, 2 * GLA_QK + GLA_V:]
    low = _dot(h, wa1_ref[...]).astype(BF16)
    gate = _dot(low, wa2_ref[...]) + ba_ref[...]
    la_ref[...] = _log_sigmoid(gate) * (1.0 / GLA_TAU)


def _gla_pre(x, g, w_in, w_a1, w_a2, b_a, tm_pref=512):
    n, d = x.shape
    tm = _row_tile(n, tm_pref)
    rank = w_a1.shape[1]
    wa1 = jnp.pad(w_a1, ((0, 0), (0, GLA_RANK_PAD - rank))).astype(BF16)
    wa2 = jnp.pad(w_a2, ((0, GLA_RANK_PAD - rank), (0, 0))).astype(BF16)
    row = lambda i: (i, 0)
    full = lambda i: (0, 0)
    widths = (GLA_QK, GLA_QK, GLA_V, GLA_V, GLA_QK)
    return pl.pallas_call(
        _gla_pre_kernel,
        grid=(n // tm,),
        in_specs=[pl.BlockSpec((tm, d), row), pl.BlockSpec((1, d), full),
                  pl.BlockSpec(w_in.shape, full), pl.BlockSpec(wa1.shape, full),
                  pl.BlockSpec(wa2.shape, full), pl.BlockSpec((1, GLA_QK), full)],
        out_specs=[pl.BlockSpec((tm, w), row) for w in widths],
        out_shape=[jax.ShapeDtypeStruct((n, w), F32) for w in widths],
        compiler_params=_cparams(("parallel",), 48),
        name="gla_pre",
    )(x, g.reshape(1, d), w_in.astype(BF16), wa1, wa2, b_a.reshape(1, GLA_QK))


def _gla_scan_kernel(q_ref, k_ref, la_ref, v_ref, r_ref, gn_ref, s0_ref,
                     mo_ref, s_ref, b_scr):
    chunk = q_ref.shape[1]
    nsub = chunk // GLA_SUB

    @pl.when(pl.program_id(1) == 0)
    def _():
        s_ref[...] = s0_ref[...]

    ri = lax.broadcasted_iota(jnp.int32, (chunk, chunk), 0)
    ci = lax.broadcasted_iota(jnp.int32, (chunk, chunk), 1)
    tri = (ri >= ci).astype(F32)
    b_scr[...] = jnp.dot(tri, la_ref[0], precision=HIGHEST, preferred_element_type=F32)
    sub_row = lax.broadcasted_iota(jnp.int32, (GLA_SUB, GLA_DK), 0)
    sub_col = lax.broadcasted_iota(jnp.int32, (GLA_SUB, chunk), 1)
    gn = gn_ref[...]

    for h in range(GLA_HEADS):
        ks = slice(h * GLA_DK, (h + 1) * GLA_DK)
        vs = slice(h * GLA_DV, (h + 1) * GLA_DV)
        q = q_ref[0, :, ks]
        k = k_ref[0, :, ks]
        v_bf = v_ref[0, :, vs].astype(BF16)
        b = b_scr[:, ks]
        b_last = b[chunk - 1:chunk, :]

        s_old = s_ref[0, h]
        o_inter = _dot((q * jnp.exp(b)).astype(BF16), s_old.astype(BF16))

        k_dec_t = (k * jnp.exp(b_last - b)).T
        decay_col = jnp.broadcast_to(jnp.exp(b_last), (GLA_DK, GLA_DK)).T
        decay = jnp.concatenate([decay_col] * (GLA_DV // GLA_DK), axis=1)
        s_ref[0, h] = decay * s_old + _dot(k_dec_t.astype(BF16), v_bf)

        for blk in range(nsub):
            r0 = blk * GLA_SUB
            rows = slice(r0, r0 + GLA_SUB)
            q_i = q[rows]
            b_i = b[rows]
            acc = o_inter[rows]
            if blk > 0:
                ref_b = b_scr[r0 - 1:r0, ks]
                q_t = (q_i * jnp.exp(b_i - ref_b)).astype(BF16)
                k_t = (k[:r0] * jnp.exp(ref_b - b[:r0])).astype(BF16)
                a_off = _dot_nt(q_t, k_t)
                acc = acc + _dot(a_off.astype(BF16), v_bf[:r0])
            diag = jnp.zeros((GLA_SUB, chunk), F32)
            for j in range(GLA_SUB):
                k_j = k_ref[0, r0 + j:r0 + j + 1, ks]
                b_j = b_scr[r0 + j:r0 + j + 1, ks]
                e = jnp.where(sub_row >= j, b_i - b_j, NEG_INF)
                w = jnp.sum(q_i * k_j * jnp.exp(e), axis=-1, keepdims=True)
                diag = jnp.where(sub_col == r0 + j, w, diag)
            acc = acc + _dot(diag.astype(BF16), v_bf)
            o_n = _rms_norm(acc, gn) * _silu(r_ref[0, rows, vs])
            mo_ref[0, rows, vs] = o_n.astype(BF16)


def _gla_scan(q, k, la, v, r, g_norm, s0):
    bsz, t, _ = q.shape
    chunk = _row_tile(t, GLA_CHUNK)
    assert chunk % GLA_SUB == 0
    qk_spec = pl.BlockSpec((1, chunk, GLA_QK), lambda b, c: (b, c, 0))
    v_spec = pl.BlockSpec((1, chunk, GLA_V), lambda b, c: (b, c, 0))
    s_spec = pl.BlockSpec((1, GLA_HEADS, GLA_DK, GLA_DV), lambda b, c: (b, 0, 0, 0))
    return pl.pallas_call(
        _gla_scan_kernel,
        grid=(bsz, t // chunk),
        in_specs=[qk_spec, qk_spec, qk_spec, v_spec, v_spec,
                  pl.BlockSpec((1, GLA_DV), lambda b, c: (0, 0)), s_spec],
        out_specs=[v_spec, s_spec],
        out_shape=[jax.ShapeDtypeStruct((bsz, t, GLA_V), BF16),
                   jax.ShapeDtypeStruct((bsz, GLA_HEADS, GLA_DK, GLA_DV), F32)],
        scratch_shapes=[pltpu.VMEM((chunk, GLA_QK), F32)],
        compiler_params=_cparams(("parallel", "arbitrary"), 32),
        name="gla_scan",
    )(q, k, la, v, r, g_norm.reshape(1, GLA_DV), s0)


def _gla_step_kernel(q_ref, k_ref, la_ref, v_ref, r_ref, gn_ref, s0_ref, mo_ref, s_ref):
    gn = gn_ref[...]
    for h in range(GLA_HEADS):
        ks = slice(h * GLA_DK, (h + 1) * GLA_DK)
        vs = slice(h * GLA_DV, (h + 1) * GLA_DV)
        q = q_ref[0, :, ks]
        k = k_ref[0, :, ks]
        a = jnp.exp(la_ref[0, :, ks])
        v = v_ref[0, :, vs]
        s_old = s0_ref[0, h]

        def col(row_vec):
            sq = jnp.broadcast_to(row_vec, (GLA_DK, GLA_DK)).T
            return jnp.concatenate([sq] * (GLA_DV // GLA_DK), axis=1)

        s_ref[0, h] = col(a) * s_old + col(k) * v
        o = jnp.sum(col(q * a) * s_old, axis=0, keepdims=True) + jnp.sum(q * k, axis=-1, keepdims=True) * v
        o_n = _rms_norm(o, gn) * _silu(r_ref[0, :, vs])
        mo_ref[0, :, vs] = o_n.astype(BF16)


def _gla_step(q, k, la, v, r, g_norm, s0):
    n = q.shape[0]
    qk_spec = pl.BlockSpec((1, 1, GLA_QK), lambda b: (b, 0, 0))
    v_spec = pl.BlockSpec((1, 1, GLA_V), lambda b: (b, 0, 0))
    s_spec = pl.BlockSpec((1, GLA_HEADS, GLA_DK, GLA_DV), lambda b: (b, 0, 0, 0))
    r3 = lambda a: a.reshape(n, 1, a.shape[-1])
    mo, s_new = pl.pallas_call(
        _gla_step_kernel,
        grid=(n,),
        in_specs=[qk_spec, qk_spec, qk_spec, v_spec, v_spec,
                  pl.BlockSpec((1, GLA_DV), lambda b: (0, 0)), s_spec],
        out_specs=[v_spec, s_spec],
        out_shape=[jax.ShapeDtypeStruct((n, 1, GLA_V), BF16),
                   jax.ShapeDtypeStruct(s0.shape, F32)],
        compiler_params=_cparams(("parallel",), 16),
        name="gla_step",
    )(r3(q), r3(k), r3(la), r3(v), r3(r), g_norm.reshape(1, GLA_DV), s0)
    return mo.reshape(n, GLA_V), s_new


def _post_kernel(x_ref, mo_ref, wo_ref, g_ref, wup_ref, wdn_ref, y_ref, h_scr):
    @pl.when(pl.program_id(1) == 0)
    def _():
        x1 = x_ref[...] + _dot(mo_ref[...], wo_ref[...])
        y_ref[...] = x1
        h_scr[...] = _rms_norm(x1, g_ref[...]).astype(BF16)

    a = jnp.maximum(_dot(h_scr[...], wup_ref[...]), 0.0)
    y_ref[...] += _dot((a * a).astype(BF16), wdn_ref[...])


def _post(x, mo, w_out, g, w_up, w_down, tm_pref=512, tf_pref=2048):
    n, d = x.shape
    d_ff = w_up.shape[1]
    tm = _row_tile(n, tm_pref)
    tf = _row_tile(d_ff, tf_pref)
    return pl.pallas_call(
        _post_kernel,
        grid=(n // tm, d_ff // tf),
        in_specs=[pl.BlockSpec((tm, d), lambda i, j: (i, 0)),
                  pl.BlockSpec((tm, d), lambda i, j: (i, 0)),
                  pl.BlockSpec((d, d), lambda i, j: (0, 0)),
                  pl.BlockSpec((1, d), lambda i, j: (0, 0)),
                  pl.BlockSpec((d, tf), lambda i, j: (0, j)),
                  pl.BlockSpec((tf, d), lambda i, j: (j, 0))],
        out_specs=pl.BlockSpec((tm, d), lambda i, j: (i, 0)),
        out_shape=jax.ShapeDtypeStruct((n, d), F32),
        scratch_shapes=[pltpu.VMEM((tm, d), BF16)],
        compiler_params=_cparams(("parallel", "arbitrary"), 48),
        name="post_mlp",
    )(x, mo, w_out.astype(BF16), g.reshape(1, d), w_up.astype(BF16), w_down.astype(BF16))


def _seg_norm(x, g, seg):
    sq = x * x
    if seg == HEAD_W:
        scale = lax.rsqrt(jnp.mean(sq, axis=-1, keepdims=True) + EPS)
    else:
        assert 2 * seg == HEAD_W
        lo = lax.broadcasted_iota(jnp.int32, x.shape, 1) < seg
        s_lo = jnp.sum(jnp.where(lo, sq, 0.0), axis=-1, keepdims=True)
        s_hi = jnp.sum(jnp.where(lo, 0.0, sq), axis=-1, keepdims=True)
        scale = lax.rsqrt(jnp.where(lo, s_lo, s_hi) * (1.0 / seg) + EPS)
    return x * scale * g


def _attn_pre_kernel(x_ref, g_ref, win_ref, qn_ref, kn_ref, *out_refs, seg, q_scale, tb, emit_qf, emit_vt):
    out_refs = list(out_refs)
    qb_ref, k_ref, v_ref = out_refs[:3]
    qf_ref = out_refs[3] if emit_qf else None
    kb_ref, vt_ref = out_refs[-2:] if emit_vt else (None, None)
    d = x_ref.shape[1]
    h = _rms_norm(x_ref[...], g_ref[...]).astype(BF16)
    y = _dot(h, win_ref[...])
    tm = y.shape[0]
    for hd in range(ATTN_HEADS):
        sl = slice(hd * HEAD_W, (hd + 1) * HEAD_W)
        qh = _seg_norm(y[:, hd * HEAD_W:(hd + 1) * HEAD_W], qn_ref[:, sl], seg)
        kh = _seg_norm(y[:, d + hd * HEAD_W:d + (hd + 1) * HEAD_W], kn_ref[:, sl], seg)
        vh = y[:, 2 * d + hd * HEAD_W:2 * d + (hd + 1) * HEAD_W]
        qb_ref[:, sl] = (qh * q_scale).astype(BF16)
        k_ref[:, sl] = kh
        v_ref[:, sl] = vh
        if emit_qf:
            qf_ref[:, sl] = qh
        if emit_vt:
            kb_ref[:, sl] = kh.astype(BF16)
            for j in range(tm // tb):
                vt_ref[0, hd, j] = vh[j * tb:(j + 1) * tb].T.astype(BF16)


def _attn_pre(x, g, w_in, q_norm, k_norm, *, seg, q_scale, emit_qf=False, batch=None, tb=None, tm_pref=512):
    n, d = x.shape
    emit_vt = batch is not None
    tm = _row_tile(n, tm_pref)
    row = lambda i: (i, 0)
    full = lambda i: (0, 0)
    reps = d // q_norm.shape[0]
    dtypes = [BF16, F32, F32] + ([F32] if emit_qf else []) + ([BF16] if emit_vt else [])
    out_specs = [pl.BlockSpec((tm, d), row) for _ in dtypes]
    out_shape = [jax.ShapeDtypeStruct((n, d), dt) for dt in dtypes]
    if emit_vt:
        t = n // batch
        assert tm % tb == 0 and t % tm == 0
        tiles = t // tm
        out_specs.append(pl.BlockSpec((1, ATTN_HEADS, tm // tb, HEAD_W, tb),
                                      lambda i: (i // tiles, 0, i % tiles, 0, 0)))
        out_shape.append(jax.ShapeDtypeStruct((batch, ATTN_HEADS, t // tb, HEAD_W, tb), BF16))
    return pl.pallas_call(
        functools.partial(_attn_pre_kernel, seg=seg, q_scale=q_scale, tb=tb, emit_qf=emit_qf, emit_vt=emit_vt),
        grid=(n // tm,),
        in_specs=[pl.BlockSpec((tm, d), row), pl.BlockSpec((1, d), full),
                  pl.BlockSpec(w_in.shape, full), pl.BlockSpec((1, d), full), pl.BlockSpec((1, d), full)],
        out_specs=out_specs,
        out_shape=out_shape,
        compiler_params=_cparams(("parallel",), 56),
        name="attn_pre",
    )(x, g.reshape(1, d), w_in.astype(BF16),
      jnp.tile(q_norm, reps).reshape(1, d), jnp.tile(k_norm, reps).reshape(1, d))


def _bias_tile_kernel(bucket_ref, rb_ref, o_ref):
    h = pl.program_id(0)
    bucket = bucket_ref[...]
    far = rb_ref[FAR_BUCKET, h]
    acc = jnp.full(bucket.shape, NEG_INF, F32)
    for bk in range(REL_BUCKETS):
        acc = jnp.where(bucket == bk, (rb_ref[bk, h] - far) * LOG2E, acc)
    o_ref[0] = acc


def _bias_tiles(rel_bias, buckets):
    return pl.pallas_call(
        _bias_tile_kernel,
        grid=(ATTN_HEADS,),
        in_specs=[pl.BlockSpec(buckets.shape, lambda h: (0, 0, 0)),
                  pl.BlockSpec(memory_space=pltpu.SMEM)],
        out_specs=pl.BlockSpec((1,) + buckets.shape, lambda h: (h, 0, 0, 0)),
        out_shape=jax.ShapeDtypeStruct((ATTN_HEADS,) + buckets.shape, F32),
        compiler_params=_cparams(("parallel",), 32),
        name="bias_tiles",
    )(jnp.asarray(buckets), rel_bias)


def _block_buckets(tb):
    j = np.arange(tb)[:, None]
    i = np.arange(tb)[None, :]
    diag = np.where(i >= j, _rel_bucket_np(i - j), -1)
    prev = _rel_bucket_np(tb + i - j)
    return np.stack([diag, prev]).astype(np.int32)


def _softmax_group(s_list, vt_list, m_ref, l_ref, acc_ref, col_max=None):
    m_prev = m_ref[...]
    m_new = m_prev
    for i, s in enumerate(s_list):
        m_new = jnp.maximum(m_new, jnp.max(s, axis=0, keepdims=True) if col_max is None else col_max[i])
    alpha = jnp.exp2(m_prev - m_new)
    l_new = alpha * l_ref[...]
    acc = alpha * acc_ref[...]
    for s, vt in zip(s_list, vt_list):
        p = jnp.exp2(s - m_new)
        l_new = l_new + jnp.sum(p, axis=0, keepdims=True)
        acc = acc + _dot(vt, p.astype(BF16))
    m_ref[...] = m_new
    l_ref[...] = l_new
    acc_ref[...] = acc


def _drain_far_groups(n_groups, qk, softmax, buf_a, buf_b):
    def pair(j, carry):
        g = 2 * j
        qk(g + 1, buf_b)
        softmax(g, buf_a)
        qk(g + 2, buf_a)
        softmax(g + 1, buf_b)
        return carry

    lax.fori_loop(0, jnp.maximum(n_groups - 1, 0) // 2, pair, 0)
    last = n_groups - 1
    odd = jnp.bitwise_and(n_groups, 1) == 1

    @pl.when(odd)
    def _():
        softmax(last, buf_a)

    @pl.when(jnp.logical_and(n_groups > 0, jnp.logical_not(odd)))
    def _():
        qk(last, buf_b)
        softmax(last - 1, buf_a)
        softmax(last, buf_b)


def _diff_attn_kernel(q_ref, k_ref, vt_ref, bias_ref, lq1_ref, lk1_ref, lq2_ref, lk2_ref, subln_ref,
                      o_ref, s_a, s_b, mx_a, mx_b, m1, l1, a1, m2, l2, a2):
    tb = q_ref.shape[1]
    qi = pl.program_id(2)
    nq = pl.num_programs(2)
    q = q_ref[0]
    lane = lax.broadcasted_iota(jnp.int32, q.shape, 1)
    q1 = jnp.where(lane < DIFF_DK, q, jnp.zeros_like(q))
    q2 = jnp.where(lane < DIFF_DK, jnp.zeros_like(q), q)

    for m_ref, l_ref, a_ref in ((m1, l1, a1), (m2, l2, a2)):
        m_ref[...] = jnp.full(m_ref.shape, NEG_INF, F32)
        l_ref[...] = jnp.zeros(l_ref.shape, F32)
        a_ref[...] = jnp.zeros(a_ref.shape, F32)

    def scores(kb):
        kblk = k_ref[0, pl.ds(pl.multiple_of(kb * tb, tb), tb), :]
        return _dot_nt(kblk, q1), _dot_nt(kblk, q2)

    def update(blocks, s_pairs, max_pairs=None):
        vts = [vt_ref[0, 0, kb] for kb in blocks]
        for mp, (m_ref, l_ref, a_ref) in enumerate(((m1, l1, a1), (m2, l2, a2))):
            _softmax_group([s[mp] for s in s_pairs], vts, m_ref, l_ref, a_ref,
                           None if max_pairs is None else [mx[mp] for mx in max_pairs])

    n_far = jnp.maximum(qi - 1, 0)
    n_groups = n_far // DIFF_FAR_GROUP

    def far_blocks(g):
        return [jnp.minimum(g * DIFF_FAR_GROUP + i, nq - 1) for i in range(DIFF_FAR_GROUP)]

    def qk_far_group(g, bufs):
        buf, mx = bufs
        for i, kb in enumerate(far_blocks(g)):
            for mp, s in enumerate(scores(kb)):
                buf[i, mp] = s
                mx[i, mp] = jnp.max(s, axis=0, keepdims=True)

    def softmax_far_group(g, bufs):
        buf, mx = bufs
        update(far_blocks(g), [(buf[i, 0], buf[i, 1]) for i in range(DIFF_FAR_GROUP)],
               [(mx[i, 0], mx[i, 1]) for i in range(DIFF_FAR_GROUP)])

    s_a, s_b = (s_a, mx_a), (s_b, mx_b)
    qk_far_group(0, s_a)

    prev = jnp.maximum(qi - 1, 0)
    d1, d2 = scores(qi)
    p1, p2 = scores(prev)
    prev_bias = jnp.where(qi > 0, bias_ref[0, 1], NEG_INF)
    update([qi, prev], [(d1 + bias_ref[0, 0], d2 + bias_ref[0, 0]), (p1 + prev_bias, p2 + prev_bias)])

    _drain_far_groups(n_groups, qk_far_group, softmax_far_group, s_a, s_b)

    for i in range(DIFF_FAR_GROUP - 1):
        kb = n_groups * DIFF_FAR_GROUP + i

        @pl.when(kb < n_far)
        def _(kb=kb):
            update([kb], [scores(kb)])

    lam = _lambda_full(lq1_ref, lk1_ref, lq2_ref, lk2_ref)
    o_t = a1[...] * (1.0 / l1[...]) - lam * (a2[...] * (1.0 / l2[...]))
    o = _rms_norm(o_t.T, subln_ref[...]) * (1.0 - DIFF_LAMBDA_INIT)
    o_ref[0] = o.astype(BF16)


def _diff_attn(qb, kb, vt, bias, lq1, lk1, lq2, lk2, subln):
    bsz, t, d = qb.shape
    tb = vt.shape[-1]
    nq = t // tb
    vec = lambda a: a.reshape(1, -1)
    vspec = lambda w: pl.BlockSpec((1, w), lambda b, h, i: (0, 0))
    stat = pltpu.VMEM((1, tb), F32)
    acc = pltpu.VMEM((HEAD_W, tb), F32)
    s_buf = pltpu.VMEM((DIFF_FAR_GROUP, 2, tb, tb), F32)
    mx_buf = pltpu.VMEM((DIFF_FAR_GROUP, 2, 1, tb), F32)
    return pl.pallas_call(
        _diff_attn_kernel,
        grid=(bsz, ATTN_HEADS, nq),
        in_specs=[pl.BlockSpec((1, tb, HEAD_W), lambda b, h, i: (b, i, h)),
                  pl.BlockSpec((1, t, HEAD_W), lambda b, h, i: (b, 0, h)),
                  pl.BlockSpec((1, 1, nq, HEAD_W, tb), lambda b, h, i: (b, h, 0, 0, 0)),
                  pl.BlockSpec((1, 2, tb, tb), lambda b, h, i: (h, 0, 0, 0)),
                  vspec(DIFF_DK), vspec(DIFF_DK), vspec(DIFF_DK), vspec(DIFF_DK), vspec(HEAD_W)],
        out_specs=pl.BlockSpec((1, tb, HEAD_W), lambda b, h, i: (b, i, h)),
        out_shape=jax.ShapeDtypeStruct((bsz, t, d), BF16),
        scratch_shapes=[s_buf, s_buf, mx_buf, mx_buf, stat, stat, acc, stat, stat, acc],
        compiler_params=_cparams(("parallel", "parallel", "arbitrary"), 56),
        name="diff_attn",
    )(qb, kb, vt, bias, vec(lq1), vec(lk1), vec(lq2), vec(lk2), vec(subln))


def _diff_decode_kernel(pt_ref, q_ref, ks_ref, vs_ref, blast_ref, bself_ref,
                        lq1_ref, lk1_ref, lq2_ref, lk2_ref, subln_ref, *rest):
    g = DEC_PAGES_PER_STEP
    k_pages = rest[:g]
    v_pages = rest[g:2 * g]
    o_ref, m1, l1, a1, m2, l2, a2 = rest[2 * g:]
    step = pl.program_id(1)
    last = pl.num_programs(1) - 1

    q = q_ref[0]
    lane = lax.broadcasted_iota(jnp.int32, q.shape, 1)
    q_maps = (jnp.where(lane < DIFF_DK, q, 0.0), jnp.where(lane < DIFF_DK, 0.0, q))
    states = ((m1, l1, a1), (m2, l2, a2))
    rr = lax.broadcasted_iota(jnp.int32, (2 * HEAD_W, 2 * HEAD_W), 0)
    cc = lax.broadcasted_iota(jnp.int32, (2 * HEAD_W, 2 * HEAD_W), 1)
    lane_sum = ((rr < HEAD_W) == (cc < HEAD_W)).astype(BF16)

    @pl.when(step == 0)
    def _():
        for qm, (m_ref, l_ref, a_ref) in zip(q_maps, states):
            s_self = jnp.sum(qm * ks_ref[0], axis=-1, keepdims=True) + bself_ref[...]
            m_ref[...] = jnp.broadcast_to(s_self, m_ref.shape)
            l_ref[...] = jnp.ones(l_ref.shape, F32)
            a_ref[...] = vs_ref[0]

    for i in range(g):
        k3 = k_pages[i][0]
        v3 = v_pages[i][0]
        prod = jnp.concatenate([k3 * q_maps[0], k3 * q_maps[1]], axis=-1)
        sums = _dot(prod.reshape(PAGE_SIZE * ATTN_HEADS, 2 * HEAD_W).astype(BF16), lane_sum)
        sums = sums.reshape(PAGE_SIZE, ATTN_HEADS, 2 * HEAD_W)
        for mp, (m_ref, l_ref, a_ref) in enumerate(states):
            s = sums[:, :, mp * HEAD_W:(mp + 1) * HEAD_W]
            if i == g - 1:
                s = s + jnp.where(step == last, blast_ref[...], 0.0)
            m_prev = m_ref[...]
            m_new = jnp.maximum(m_prev, jnp.max(s, axis=0))
            alpha = jnp.exp2(m_prev - m_new)
            p = jnp.exp2(s - m_new)
            l_ref[...] = alpha * l_ref[...] + jnp.sum(p, axis=0)
            a_ref[...] = alpha * a_ref[...] + jnp.sum(p * v3, axis=0)
            m_ref[...] = m_new

    @pl.when(step == last)
    def _():
        lam = _lambda_full(lq1_ref, lk1_ref, lq2_ref, lk2_ref)
        o = a1[...] * (1.0 / l1[...]) - lam * (a2[...] * (1.0 / l2[...]))
        o_ref[0] = (_rms_norm(o, subln_ref[...]) * (1.0 - DIFF_LAMBDA_INIT)).astype(BF16)


def _decode_bias(rel_bias):
    far = rel_bias[FAR_BUCKET]
    newest = (rel_bias[_rel_bucket_np(PAGE_SIZE - np.arange(PAGE_SIZE))] - far) * LOG2E
    own = (rel_bias[0] - far) * LOG2E
    return newest[:, :, None], own[:, None]


def _diff_decode(page_table, q, k_self, v_self, cache_k, cache_v, rel_bias, lq1, lk1, lq2, lk2, subln):
    n = q.shape[0]
    n_pages = page_table.shape[1]
    g = DEC_PAGES_PER_STEP
    assert n_pages % g == 0
    b_last, b_self = _decode_bias(rel_bias)
    per_head = lambda a: a.reshape(n, ATTN_HEADS, HEAD_W)
    vec = lambda a: a.reshape(1, -1)
    const = lambda shape: pl.BlockSpec(shape, lambda b, j, pt: (0,) * len(shape))
    seq = pl.BlockSpec((1, ATTN_HEADS, HEAD_W), lambda b, j, pt: (b, 0, 0))

    def page_spec(i):
        return pl.BlockSpec((1, PAGE_SIZE, ATTN_HEADS, HEAD_W),
                            lambda b, j, pt: (pt[b * n_pages + j * g + i], 0, 0, 0))

    stat = pltpu.VMEM((ATTN_HEADS, HEAD_W), F32)
    acc = pltpu.VMEM((ATTN_HEADS, HEAD_W), F32)
    out = pl.pallas_call(
        _diff_decode_kernel,
        grid_spec=pltpu.PrefetchScalarGridSpec(
            num_scalar_prefetch=1,
            grid=(n, n_pages // g),
            in_specs=[seq, seq, seq, const((PAGE_SIZE, ATTN_HEADS, 1)), const((ATTN_HEADS, 1)),
                      const((1, DIFF_DK)), const((1, DIFF_DK)), const((1, DIFF_DK)), const((1, DIFF_DK)),
                      const((1, HEAD_W))]
                     + [page_spec(i) for i in range(g)] + [page_spec(i) for i in range(g)],
            out_specs=seq,
            scratch_shapes=[stat, stat, acc, stat, stat, acc]),
        out_shape=jax.ShapeDtypeStruct((n, ATTN_HEADS, HEAD_W), BF16),
        compiler_params=_cparams(("parallel", "arbitrary"), 48),
        name="diff_decode",
    )(page_table.reshape(-1), per_head(q), per_head(k_self), per_head(v_self), b_last, b_self,
      vec(lq1), vec(lk1), vec(lq2), vec(lk2), vec(subln),
      *([cache_k] * g), *([cache_v] * g))
    return out.reshape(n, ATTN_HEADS * HEAD_W)


def _block_mean_kernel(k_ref, o_ref):
    for i in range(o_ref.shape[0]):
        o_ref[i:i + 1, :] = jnp.mean(k_ref[i * MOBA_BLOCK:(i + 1) * MOBA_BLOCK, :], axis=0, keepdims=True)


def _block_mean(k):
    n, d = k.shape
    nb = n // MOBA_BLOCK
    per = _row_tile(nb, 8)
    return pl.pallas_call(
        _block_mean_kernel,
        grid=(nb // per,),
        in_specs=[pl.BlockSpec((per * MOBA_BLOCK, d), lambda i: (i, 0))],
        out_specs=pl.BlockSpec((per, d), lambda i: (i, 0)),
        out_shape=jax.ShapeDtypeStruct((nb, d), F32),
        compiler_params=_cparams(("parallel",), 32),
        name="moba_block_mean",
    )(k)


def _topk_rank(gate, blk_idx, nb):
    rank = jnp.zeros(gate.shape, F32)
    for m in range(nb):
        gm = gate[m:m + 1, :]
        beats = (gm > gate) | ((gm == gate) & (blk_idx > m))
        rank = rank + beats.astype(F32)
    return rank


def _moba_attn_kernel(qf_ref, qb_ref, k_ref, vt_ref, kmean_ref, bias_ref, o_ref,
                      sel_scr, s_a, s_b, mx_a, mx_b, m_scr, l_scr, acc_scr):
    tb = qf_ref.shape[1]
    nb = kmean_ref.shape[1]
    qi = pl.program_id(2)
    q = qb_ref[0]

    gate = _dot_nt(kmean_ref[0], qf_ref[0], precision=HIGHEST)
    blk_idx = lax.broadcasted_iota(jnp.int32, gate.shape, 0)
    past = blk_idx < qi
    gate = jnp.where(past, gate, NEG_INF)
    rank = _topk_rank(gate, blk_idx, nb)
    sel_scr[...] = (past & (rank < MOBA_TOPK)).astype(F32)

    m_scr[...] = jnp.full(m_scr.shape, NEG_INF, F32)
    l_scr[...] = jnp.zeros(l_scr.shape, F32)
    acc_scr[...] = jnp.zeros(acc_scr.shape, F32)

    nq = pl.num_programs(2)

    def scores(kb, bias, keep):
        kblk = k_ref[0, pl.ds(pl.multiple_of(kb * tb, tb), tb), :]
        s = _dot_nt(kblk, q)
        if bias is not None:
            s = s + bias
        if keep is not None:
            s = jnp.where(keep > 0.0, s, NEG_INF)
        return s

    n_far = jnp.maximum(qi - 1, 0)
    n_groups = (n_far + MOBA_FAR_GROUP - 1) // MOBA_FAR_GROUP

    def far_blocks(g):
        return [jnp.minimum(g * MOBA_FAR_GROUP + i, nq - 1) for i in range(MOBA_FAR_GROUP)]

    def qk_far_group(g, bufs):
        buf, mx = bufs
        for i, kb in enumerate(far_blocks(g)):
            keep = jnp.where(g * MOBA_FAR_GROUP + i < n_far, sel_scr[pl.ds(kb, 1), :], 0.0)
            s = scores(kb, None, keep)
            buf[i] = s
            mx[i] = jnp.max(s, axis=0, keepdims=True)

    def softmax_far_group(g, bufs):
        buf, mx = bufs
        _softmax_group([buf[i] for i in range(MOBA_FAR_GROUP)], [vt_ref[0, 0, kb] for kb in far_blocks(g)],
                       m_scr, l_scr, acc_scr, [mx[i] for i in range(MOBA_FAR_GROUP)])

    s_a, s_b = (s_a, mx_a), (s_b, mx_b)
    qk_far_group(0, s_a)

    prev = jnp.maximum(qi - 1, 0)
    _softmax_group([scores(qi, bias_ref[0, 0], None), scores(prev, bias_ref[0, 1], sel_scr[pl.ds(prev, 1), :])],
                   [vt_ref[0, 0, qi], vt_ref[0, 0, prev]], m_scr, l_scr, acc_scr)

    _drain_far_groups(n_groups, qk_far_group, softmax_far_group, s_a, s_b)
    o_ref[0] = (acc_scr[...] * (1.0 / l_scr[...])).T.astype(BF16)


def _moba_attn(qf, qb, kb, vt, kmean, bias):
    bsz, t, d = qb.shape
    tb = vt.shape[-1]
    assert tb == MOBA_BLOCK
    nq = t // tb
    nb = kmean.shape[1]
    qspec = pl.BlockSpec((1, tb, HEAD_W), lambda b, h, i: (b, i, h))
    s_buf = pltpu.VMEM((MOBA_FAR_GROUP, tb, tb), F32)
    mx_buf = pltpu.VMEM((MOBA_FAR_GROUP, 1, tb), F32)
    return pl.pallas_call(
        _moba_attn_kernel,
        grid=(bsz, ATTN_HEADS, nq),
        in_specs=[qspec, qspec,
                  pl.BlockSpec((1, t, HEAD_W), lambda b, h, i: (b, 0, h)),
                  pl.BlockSpec((1, 1, nq, HEAD_W, tb), lambda b, h, i: (b, h, 0, 0, 0)),
                  pl.BlockSpec((1, nb, HEAD_W), lambda b, h, i: (b, 0, h)),
                  pl.BlockSpec((1, 2, tb, tb), lambda b, h, i: (h, 0, 0, 0))],
        out_specs=qspec,
        out_shape=jax.ShapeDtypeStruct((bsz, t, d), BF16),
        scratch_shapes=[pltpu.VMEM((nb, tb), F32), s_buf, s_buf, mx_buf, mx_buf,
                        pltpu.VMEM((1, tb), F32), pltpu.VMEM((1, tb), F32),
                        pltpu.VMEM((HEAD_W, tb), F32)],
        compiler_params=_cparams(("parallel", "parallel", "arbitrary"), 32),
        name="moba_attn",
    )(qf, qb, kb, vt, kmean, bias)


def _moba_route_kernel(pt_ref, qf_ref, *rest):
    g = DEC_PAGES_PER_STEP
    k_pages = rest[:g]
    sel_ref, kmean_scr = rest[g:]
    step = pl.program_id(1)
    pages_per_block = MOBA_BLOCK // PAGE_SIZE
    blocks_per_step = g // pages_per_block
    nb = kmean_scr.shape[0]
    for i in range(blocks_per_step):
        tot = jnp.sum(k_pages[i * pages_per_block][0], axis=0)
        for p in range(1, pages_per_block):
            tot = tot + jnp.sum(k_pages[i * pages_per_block + p][0], axis=0)
        kmean_scr[step * blocks_per_step + i] = tot * (1.0 / MOBA_BLOCK)

    @pl.when(step == pl.num_programs(1) - 1)
    def _():
        gate = jnp.sum(kmean_scr[...] * qf_ref[0], axis=-1, keepdims=True)
        blk_idx = lax.broadcasted_iota(jnp.int32, gate.shape, 0)
        rank = _topk_rank(gate, blk_idx, nb)
        for t in range(MOBA_TOPK):
            chosen = jnp.sum(jnp.where(rank == t, blk_idx, 0), axis=0)
            sel_ref[0, t] = jnp.broadcast_to(chosen, sel_ref.shape[2:])


def _moba_route(page_table, qf, cache_k):
    n = qf.shape[0]
    n_pages = page_table.shape[1]
    g = DEC_PAGES_PER_STEP
    nb = n_pages * PAGE_SIZE // MOBA_BLOCK
    assert n_pages % g == 0 and g % (MOBA_BLOCK // PAGE_SIZE) == 0 and nb >= MOBA_TOPK

    def page_spec(i):
        return pl.BlockSpec((1, PAGE_SIZE, ATTN_HEADS, HEAD_W),
                            lambda b, j, pt: (pt[b * n_pages + j * g + i], 0, 0, 0))

    return pl.pallas_call(
        _moba_route_kernel,
        grid_spec=pltpu.PrefetchScalarGridSpec(
            num_scalar_prefetch=1,
            grid=(n, n_pages // g),
            in_specs=[pl.BlockSpec((1, ATTN_HEADS, HEAD_W), lambda b, j, pt: (b, 0, 0))]
                     + [page_spec(i) for i in range(g)],
            out_specs=pl.BlockSpec((1, MOBA_TOPK, ATTN_HEADS, HEAD_W), lambda b, j, pt: (b, 0, 0, 0)),
            scratch_shapes=[pltpu.VMEM((nb, ATTN_HEADS, HEAD_W), F32)]),
        out_shape=jax.ShapeDtypeStruct((n, MOBA_TOPK, ATTN_HEADS, HEAD_W), jnp.int32),
        compiler_params=_cparams(("parallel", "arbitrary"), 32),
        name="moba_route",
    )(page_table.reshape(-1), qf.reshape(n, ATTN_HEADS, HEAD_W), *([cache_k] * g))


def _moba_decode_kernel(pt_ref, sel_ref, q_ref, ks_ref, vs_ref, bias_ref, ck_ref, cv_ref, o_ref,
                        k_buf, v_buf, sems, *, n_pages, newest_block):
    pages_per_block = MOBA_BLOCK // PAGE_SIZE
    n_sel = MOBA_TOPK * pages_per_block
    b = pl.program_id(0)

    def block_of(h, t):
        return sel_ref[(b * MOBA_TOPK + t) * ATTN_HEADS + h]

    def copies(h, i):
        page = pt_ref[b * n_pages + block_of(h, i // pages_per_block) * pages_per_block + i % pages_per_block]
        return (pltpu.make_async_copy(ck_ref.at[page, :, h, :], k_buf.at[h, i], sems.at[0, h, i]),
                pltpu.make_async_copy(cv_ref.at[page, :, h, :], v_buf.at[h, i], sems.at[1, h, i]))

    for h in range(ATTN_HEADS):
        for i in range(n_sel):
            for cp in copies(h, i):
                cp.start()

    for h in range(ATTN_HEADS):
        for i in range(n_sel):
            for cp in copies(h, i):
                cp.wait()

    for h in range(ATTN_HEADS):
        q = q_ref[0, h:h + 1, :]
        q8 = jnp.broadcast_to(q, (8, HEAD_W)).astype(BF16)
        s_self = jnp.sum(q * ks_ref[0, h:h + 1, :], axis=-1, keepdims=True) + bias_ref[h, 1:2, 0:1]
        k_all = k_buf[h].reshape(n_sel * PAGE_SIZE, HEAD_W).astype(BF16)
        v_all = v_buf[h].reshape(n_sel * PAGE_SIZE, HEAD_W).astype(BF16)
        near = [jnp.where((i % pages_per_block == pages_per_block - 1) & (block_of(h, i // pages_per_block) == newest_block),
                          bias_ref[h, 0:1, :], 0.0) for i in range(n_sel)]
        s = _dot_nt(q8, k_all)[0:1] + jnp.concatenate(near, axis=-1)
        m = jnp.maximum(s_self, jnp.max(s, axis=-1, keepdims=True))
        p_self = jnp.exp2(s_self - m)
        p = jnp.exp2(s - m)
        l = p_self + jnp.sum(p, axis=-1, keepdims=True)
        o = p_self * vs_ref[0, h:h + 1, :] + _dot(jnp.broadcast_to(p, (8, p.shape[1])).astype(BF16), v_all)[0:1]
        o_ref[0, h:h + 1, :] = (o * (1.0 / l)).astype(BF16)


def _moba_decode(page_table, sel, q, k_self, v_self, cache_k, cache_v, rel_bias):
    n = q.shape[0]
    n_pages = page_table.shape[1]
    pages_per_block = MOBA_BLOCK // PAGE_SIZE
    n_sel = MOBA_TOPK * pages_per_block
    b_last, b_self = _decode_bias(rel_bias)
    bias = jnp.stack([b_last[:, :, 0].T, jnp.broadcast_to(b_self, (ATTN_HEADS, PAGE_SIZE))], axis=1)
    per_head = lambda a: a.reshape(n, ATTN_HEADS, HEAD_W)
    seq = pl.BlockSpec((1, ATTN_HEADS, HEAD_W), lambda b, pt, sl: (b, 0, 0))
    page_buf = pltpu.VMEM((ATTN_HEADS, n_sel, PAGE_SIZE, HEAD_W), F32)
    out = pl.pallas_call(
        functools.partial(_moba_decode_kernel, n_pages=n_pages, newest_block=n_pages // pages_per_block - 1),
        grid_spec=pltpu.PrefetchScalarGridSpec(
            num_scalar_prefetch=2,
            grid=(n,),
            in_specs=[seq, seq, seq, pl.BlockSpec(bias.shape, lambda b, pt, sl: (0, 0, 0)),
                      pl.BlockSpec(memory_space=pl.ANY), pl.BlockSpec(memory_space=pl.ANY)],
            out_specs=seq,
            scratch_shapes=[page_buf, page_buf, pltpu.SemaphoreType.DMA((2, ATTN_HEADS, n_sel))]),
        out_shape=jax.ShapeDtypeStruct((n, ATTN_HEADS, HEAD_W), BF16),
        compiler_params=_cparams(("arbitrary",), 32),
        name="moba_decode",
    )(page_table.reshape(-1), sel[:, :, :, 0].reshape(-1),
      per_head(q), per_head(k_self), per_head(v_self), bias, cache_k, cache_v)
    return out.reshape(n, ATTN_HEADS * HEAD_W)


def kernel(x_prompt, x_sample, state_gla_l0, cache_k_l1, cache_v_l1, cache_k_l2, cache_v_l2, state_gla_l3, page_table, rel_bias, norm_mix, norm_mlp, mlp_w_up, mlp_w_down, gla_w_in, gla_w_a1, gla_w_a2, gla_b_a, gla_norm, gla_w_out, diff_w_in, diff_q_norm, diff_k_norm, diff_lq1, diff_lk1, diff_lq2, diff_lk2, diff_subln, diff_w_out, moba_w_in, moba_q_norm, moba_k_norm, moba_w_out):
    bsz, t, d = x_prompt.shape
    n_dec = x_sample.shape[0]
    assert d == D_MODEL and x_sample.shape[1] == 1
    xp = x_prompt.reshape(bsz * t, d)
    xs = x_sample.reshape(n_dec, d)
    seq = lambda a: a.reshape(bsz, t, a.shape[-1])
    heads = lambda a: a.reshape(a.shape[:-1] + (ATTN_HEADS, HEAD_W))

    def mlp(x, mo, w_out, li):
        return _post(x, mo, w_out, norm_mlp[li], mlp_w_up[li], mlp_w_down[li])

    def gla_layer(xp, xs, li, gi, state_dec):
        w = (gla_w_in[gi], gla_w_a1[gi], gla_w_a2[gi], gla_b_a[gi])
        q, k, v, r, la = _gla_pre(xp, norm_mix[li], *w)
        s0 = jnp.zeros((bsz,) + state_dec.shape[1:], F32)
        mo_p, st_p = _gla_scan(seq(q), seq(k), seq(la), seq(v), seq(r), gla_norm[gi], s0)
        q, k, v, r, la = _gla_pre(xs, norm_mix[li], *w)
        mo_s, st_s = _gla_step(q, k, la, v, r, gla_norm[gi], state_dec)
        xp = mlp(xp, mo_p.reshape(bsz * t, d), gla_w_out[gi], li)
        xs = mlp(xs, mo_s, gla_w_out[gi], li)
        return xp, xs, st_p, st_s

    xp, xs, st_p0, st_s0 = gla_layer(xp, xs, 0, 0, state_gla_l0)

    tb = min(DIFF_TB, t)
    lam_w = (diff_lq1, diff_lk1, diff_lq2, diff_lk2, diff_subln)
    pre = dict(seg=DIFF_DK, q_scale=DIFF_DK ** -0.5 * LOG2E)
    qb, k, v, kb, vt = _attn_pre(xp, norm_mix[1], diff_w_in, diff_q_norm, diff_k_norm, batch=bsz, tb=tb, **pre)
    bias = _bias_tiles(rel_bias, _block_buckets(tb))
    mo_p = _diff_attn(seq(qb), seq(kb), vt, bias, *lam_w)
    k_p1, v_p1 = heads(seq(k)), heads(seq(v))
    qb, k, v = _attn_pre(xs, norm_mix[1], diff_w_in, diff_q_norm, diff_k_norm, **pre)
    mo_s = _diff_decode(page_table, qb.astype(F32), k, v, cache_k_l1, cache_v_l1, rel_bias, *lam_w)
    k_s1, v_s1 = heads(k.reshape(n_dec, 1, d)), heads(v.reshape(n_dec, 1, d))
    xp = mlp(xp, mo_p.reshape(bsz * t, d), diff_w_out, 1)
    xs = mlp(xs, mo_s, diff_w_out, 1)

    pre = dict(seg=HEAD_W, q_scale=HEAD_W ** -0.5 * LOG2E, emit_qf=True)
    qb, k, v, qf, kb, vt = _attn_pre(xp, norm_mix[2], moba_w_in, moba_q_norm, moba_k_norm, batch=bsz, tb=MOBA_TB, **pre)
    kmean = _block_mean(k).reshape(bsz, t // MOBA_BLOCK, d)
    bias = _bias_tiles(rel_bias, _block_buckets(MOBA_TB))
    mo_p = _moba_attn(seq(qf), seq(qb), seq(kb), vt, kmean, bias)
    k_p2, v_p2 = heads(seq(k)), heads(seq(v))
    qb, k, v, qf = _attn_pre(xs, norm_mix[2], moba_w_in, moba_q_norm, moba_k_norm, **pre)
    sel = _moba_route(page_table, qf, cache_k_l2)
    mo_s = _moba_decode(page_table, sel, qb.astype(F32), k, v, cache_k_l2, cache_v_l2, rel_bias)
    k_s2, v_s2 = heads(k.reshape(n_dec, 1, d)), heads(v.reshape(n_dec, 1, d))
    xp = mlp(xp, mo_p.reshape(bsz * t, d), moba_w_out, 2)
    xs = mlp(xs, mo_s, moba_w_out, 2)

    xp, xs, st_p3, st_s3 = gla_layer(xp, xs, 3, 1, state_gla_l3)

    return (xp.reshape(bsz, t, d), xs.reshape(n_dec, 1, d), st_p0, st_s0,
            k_p1, v_p1, k_s1, v_s1, k_p2, v_p2, k_s2, v_s2, st_p3, st_s3)
```

```python
import functools
import math

import numpy as np
import jax
import jax.numpy as jnp
from jax import lax
from jax.experimental import pallas as pl
from jax.experimental.pallas import tpu as pltpu

F32 = jnp.float32
BF16 = jnp.bfloat16
NEG_INF = float("-inf")
HIGHEST = lax.Precision.HIGHEST

EPS = 1e-6
D_MODEL = 1024
PAGE_SIZE = 128

GLA_HEADS = 4
GLA_DK = 128
GLA_DV = 256
GLA_QK = GLA_HEADS * GLA_DK
GLA_V = GLA_HEADS * GLA_DV
GLA_TAU = 16.0
GLA_RANK_PAD = 128
GLA_CHUNK = 128
GLA_SUB = 16

ATTN_HEADS = 8
HEAD_W = 128
DIFF_DK = 64
DIFF_LAMBDA_INIT = 0.8 - 0.6 * math.exp(-0.3 * 1)
MOBA_BLOCK = 256
MOBA_TOPK = 3
DIFF_TB = 512
MOBA_TB = MOBA_BLOCK
DIFF_FAR_GROUP = 2
MOBA_FAR_GROUP = 4
LOG2E = math.log2(math.e)

REL_BUCKETS = 32
REL_MAX_EXACT = 16
REL_MAX_DIST = 128
FAR_BUCKET = REL_BUCKETS - 1

V7X_VMEM_BYTES = 64 * 1024 * 1024
MIB = 1024 * 1024
DEC_PAGES_PER_STEP = 8

NT_DIMS = (((1,), (1,)), ((), ()))


def _cparams(sem, vmem_mib):
    assert vmem_mib * MIB < V7X_VMEM_BYTES
    return pltpu.CompilerParams(dimension_semantics=sem, vmem_limit_bytes=vmem_mib * MIB)


def _rel_bucket_np(dist):
    dist = np.maximum(dist, 0)
    df = np.maximum(dist, 1).astype(np.float32)
    large = REL_MAX_EXACT + (np.log(df / np.float32(REL_MAX_EXACT)) / np.float32(math.log(REL_MAX_DIST / REL_MAX_EXACT))
                             * np.float32(REL_BUCKETS - REL_MAX_EXACT)).astype(np.int32)
    large = np.minimum(large, REL_BUCKETS - 1)
    return np.where(dist < REL_MAX_EXACT, dist, large).astype(np.int32)


def _rms_norm(x, g):
    return x * lax.rsqrt(jnp.mean(x * x, axis=-1, keepdims=True) + EPS) * g


def _log_sigmoid(x):
    return jnp.minimum(x, 0.0) - jnp.log1p(jnp.exp(-jnp.abs(x)))


def _silu(x):
    return x * (1.0 / (1.0 + jnp.exp(-x)))


def _dot(a, b):
    return jnp.dot(a, b, preferred_element_type=F32)


def _dot_nt(a, b, precision=None):
    return lax.dot_general(a, b, NT_DIMS, precision=precision, preferred_element_type=F32)


def _lambda_full(lq1_ref, lk1_ref, lq2_ref, lk2_ref):
    return (jnp.exp(jnp.sum(lq1_ref[...] * lk1_ref[...], axis=-1, keepdims=True))
            - jnp.exp(jnp.sum(lq2_ref[...] * lk2_ref[...], axis=-1, keepdims=True))
            + DIFF_LAMBDA_INIT)


def _row_tile(n, pref):
    t = min(n, pref)
    assert n % t == 0
    return t


def _gla_pre_kernel(x_ref, g_ref, win_ref, wa1_ref, wa2_ref, ba_ref,
                    q_ref, k_ref, v_ref, r_ref, la_ref):
    h = _rms_norm(x_ref[...], g_ref[...]).astype(BF16)
    y = _dot(h, win_ref[...])
    q_ref[...] = y[:, :GLA_QK] * (GLA_DK ** -0.5)
    k_ref[...] = y[:, GLA_QK:2 * GLA_QK]
    v_ref[...] = y[:, 2 * GLA_QK:2 * GLA_QK + GLA_V]
    r_ref[...] = y[:, 2 * GLA_QK + GLA_V:]
    low = _dot(h, wa1_ref[...]).astype(BF16)
    gate = _dot(low, wa2_ref[...]) + ba_ref[...]
    la_ref[...] = _log_sigmoid(gate) * (1.0 / GLA_TAU)


def _gla_pre(x, g, w_in, w_a1, w_a2, b_a, tm_pref=512):
    n, d = x.shape
    tm = _row_tile(n, tm_pref)
    rank = w_a1.shape[1]
    wa1 = jnp.pad(w_a1, ((0, 0), (0, GLA_RANK_PAD - rank))).astype(BF16)
    wa2 = jnp.pad(w_a2, ((0, GLA_RANK_PAD - rank), (0, 0))).astype(BF16)
    row = lambda i: (i, 0)
    full = lambda i: (0, 0)
    widths = (GLA_QK, GLA_QK, GLA_V, GLA_V, GLA_QK)
    return pl.pallas_call(
        _gla_pre_kernel,
        grid=(n // tm,),
        in_specs=[pl.BlockSpec((tm, d), row), pl.BlockSpec((1, d), full),
                  pl.BlockSpec(w_in.shape, full), pl.BlockSpec(wa1.shape, full),
                  pl.BlockSpec(wa2.shape, full), pl.BlockSpec((1, GLA_QK), full)],
        out_specs=[pl.BlockSpec((tm, w), row) for w in widths],
        out_shape=[jax.ShapeDtypeStruct((n, w), F32) for w in widths],
        compiler_params=_cparams(("parallel",), 48),
        name="gla_pre",
    )(x, g.reshape(1, d), w_in.astype(BF16), wa1, wa2, b_a.reshape(1, GLA_QK))


def _gla_scan_kernel(q_ref, k_ref, la_ref, v_ref, r_ref, gn_ref, s0_ref,
                     mo_ref, s_ref, b_scr):
    chunk = q_ref.shape[1]
    nsub = chunk // GLA_SUB

    @pl.when(pl.program_id(1) == 0)
    def _():
        s_ref[...] = s0_ref[...]

    ri = lax.broadcasted_iota(jnp.int32, (chunk, chunk), 0)
    ci = lax.broadcasted_iota(jnp.int32, (chunk, chunk), 1)
    tri = (ri >= ci).astype(F32)
    b_scr[...] = jnp.dot(tri, la_ref[0], precision=HIGHEST, preferred_element_type=F32)
    sub_row = lax.broadcasted_iota(jnp.int32, (GLA_SUB, GLA_DK), 0)
    sub_col = lax.broadcasted_iota(jnp.int32, (GLA_SUB, chunk), 1)
    gn = gn_ref[...]

    for h in range(GLA_HEADS):
        ks = slice(h * GLA_DK, (h + 1) * GLA_DK)
        vs = slice(h * GLA_DV, (h + 1) * GLA_DV)
        q = q_ref[0, :, ks]
        k = k_ref[0, :, ks]
        v_bf = v_ref[0, :, vs].astype(BF16)
        b = b_scr[:, ks]
        b_last = b[chunk - 1:chunk, :]

        s_old = s_ref[0, h]
        o_inter = _dot((q * jnp.exp(b)).astype(BF16), s_old.astype(BF16))

        k_dec_t = (k * jnp.exp(b_last - b)).T
        decay_col = jnp.broadcast_to(jnp.exp(b_last), (GLA_DK, GLA_DK)).T
        decay = jnp.concatenate([decay_col] * (GLA_DV // GLA_DK), axis=1)
        s_ref[0, h] = decay * s_old + _dot(k_dec_t.astype(BF16), v_bf)

        for blk in range(nsub):
            r0 = blk * GLA_SUB
            rows = slice(r0, r0 + GLA_SUB)
            q_i = q[rows]
            b_i = b[rows]
            acc = o_inter[rows]
            if blk > 0:
                ref_b = b_scr[r0 - 1:r0, ks]
                q_t = (q_i * jnp.exp(b_i - ref_b)).astype(BF16)
                k_t = (k[:r0] * jnp.exp(ref_b - b[:r0])).astype(BF16)
                a_off = _dot_nt(q_t, k_t)
                acc = acc + _dot(a_off.astype(BF16), v_bf[:r0])
            diag = jnp.zeros((GLA_SUB, chunk), F32)
            for j in range(GLA_SUB):
                k_j = k_ref[0, r0 + j:r0 + j + 1, ks]
                b_j = b_scr[r0 + j:r0 + j + 1, ks]
                e = jnp.where(sub_row >= j, b_i - b_j, NEG_INF)
                w = jnp.sum(q_i * k_j * jnp.exp(e), axis=-1, keepdims=True)
                diag = jnp.where(sub_col == r0 + j, w, diag)
            acc = acc + _dot(diag.astype(BF16), v_bf)
            o_n = _rms_norm(acc, gn) * _silu(r_ref[0, rows, vs])
            mo_ref[0, rows, vs] = o_n.astype(BF16)


def _gla_scan(q, k, la, v, r, g_norm, s0):
    bsz, t, _ = q.shape
    chunk = _row_tile(t, GLA_CHUNK)
    assert chunk % GLA_SUB == 0
    qk_spec = pl.BlockSpec((1, chunk, GLA_QK), lambda b, c: (b, c, 0))
    v_spec = pl.BlockSpec((1, chunk, GLA_V), lambda b, c: (b, c, 0))
    s_spec = pl.BlockSpec((1, GLA_HEADS, GLA_DK, GLA_DV), lambda b, c: (b, 0, 0, 0))
    return pl.pallas_call(
        _gla_scan_kernel,
        grid=(bsz, t // chunk),
        in_specs=[qk_spec, qk_spec, qk_spec, v_spec, v_spec,
                  pl.BlockSpec((1, GLA_DV), lambda b, c: (0, 0)), s_spec],
        out_specs=[v_spec, s_spec],
        out_shape=[jax.ShapeDtypeStruct((bsz, t, GLA_V), BF16),
                   jax.ShapeDtypeStruct((bsz, GLA_HEADS, GLA_DK, GLA_DV), F32)],
        scratch_shapes=[pltpu.VMEM((chunk, GLA_QK), F32)],
        compiler_params=_cparams(("parallel", "arbitrary"), 32),
        name="gla_scan",
    )(q, k, la, v, r, g_norm.reshape(1, GLA_DV), s0)


def _gla_step_kernel(q_ref, k_ref, la_ref, v_ref, r_ref, gn_ref, s0_ref, mo_ref, s_ref):
    gn = gn_ref[...]
    for h in range(GLA_HEADS):
        ks = slice(h * GLA_DK, (h + 1) * GLA_DK)
        vs = slice(h * GLA_DV, (h + 1) * GLA_DV)
        q = q_ref[0, :, ks]
        k = k_ref[0, :, ks]
        a = jnp.exp(la_ref[0, :, ks])
        v = v_ref[0, :, vs]
        s_old = s0_ref[0, h]

        def col(row_vec):
            sq = jnp.broadcast_to(row_vec, (GLA_DK, GLA_DK)).T
            return jnp.concatenate([sq] * (GLA_DV // GLA_DK), axis=1)

        s_ref[0, h] = col(a) * s_old + col(k) * v
        o = jnp.sum(col(q * a) * s_old, axis=0, keepdims=True) + jnp.sum(q * k, axis=-1, keepdims=True) * v
        o_n = _rms_norm(o, gn) * _silu(r_ref[0, :, vs])
        mo_ref[0, :, vs] = o_n.astype(BF16)


def _gla_step(q, k, la, v, r, g_norm, s0):
    n = q.shape[0]
    qk_spec = pl.BlockSpec((1, 1, GLA_QK), lambda b: (b, 0, 0))
    v_spec = pl.BlockSpec((1, 1, GLA_V), lambda b: (b, 0, 0))
    s_spec = pl.BlockSpec((1, GLA_HEADS, GLA_DK, GLA_DV), lambda b: (b, 0, 0, 0))
    r3 = lambda a: a.reshape(n, 1, a.shape[-1])
    mo, s_new = pl.pallas_call(
        _gla_step_kernel,
        grid=(n,),
        in_specs=[qk_spec, qk_spec, qk_spec, v_spec, v_spec,
                  pl.BlockSpec((1, GLA_DV), lambda b: (0, 0)), s_spec],
        out_specs=[v_spec, s_spec],
        out_shape=[jax.ShapeDtypeStruct((n, 1, GLA_V), BF16),
                   jax.ShapeDtypeStruct(s0.shape, F32)],
        compiler_params=_cparams(("parallel",), 16),
        name="gla_step",
    )(r3(q), r3(k), r3(la), r3(v), r3(r), g_norm.reshape(1, GLA_DV), s0)
    return mo.reshape(n, GLA_V), s_new


def _post_kernel(x_ref, mo_ref, wo_ref, g_ref, wup_ref, wdn_ref, y_ref, h_scr):
    @pl.when(pl.program_id(1) == 0)
    def _():
        x1 = x_ref[...] + _dot(mo_ref[...], wo_ref[...])
        y_ref[...] = x1
        h_scr[...] = _rms_norm(x1, g_ref[...]).astype(BF16)

    a = jnp.maximum(_dot(h_scr[...], wup_ref[...]), 0.0)
    y_ref[...] += _dot((a * a).astype(BF16), wdn_ref[...])


def _post(x, mo, w_out, g, w_up, w_down, tm_pref=256, tf_pref=4096):
    n, d = x.shape
    d_ff = w_up.shape[1]
    tm = _row_tile(n, tm_pref)
    tf = _row_tile(d_ff, tf_pref)
    return pl.pallas_call(
        _post_kernel,
        grid=(n // tm, d_ff // tf),
        in_specs=[pl.BlockSpec((tm, d), lambda i, j: (i, 0)),
                  pl.BlockSpec((tm, d), lambda i, j: (i, 0)),
                  pl.BlockSpec((d, d), lambda i, j: (0, 0)),
                  pl.BlockSpec((1, d), lambda i, j: (0, 0)),
                  pl.BlockSpec((d, tf), lambda i, j: (0, j)),
                  pl.BlockSpec((tf, d), lambda i, j: (j, 0))],
        out_specs=pl.BlockSpec((tm, d), lambda i, j: (i, 0)),
        out_shape=jax.ShapeDtypeStruct((n, d), F32),
        scratch_shapes=[pltpu.VMEM((tm, d), BF16)],
        compiler_params=_cparams(("parallel", "arbitrary"), 48),
        name="post_mlp",
    )(x, mo, w_out.astype(BF16), g.reshape(1, d), w_up.astype(BF16), w_down.astype(BF16))


def _seg_norm(x, g, seg):
    sq = x * x
    if seg == HEAD_W:
        scale = lax.rsqrt(jnp.mean(sq, axis=-1, keepdims=True) + EPS)
    else:
        assert 2 * seg == HEAD_W
        lo = lax.broadcasted_iota(jnp.int32, x.shape, 1) < seg
        s_lo = jnp.sum(jnp.where(lo, sq, 0.0), axis=-1, keepdims=True)
        s_hi = jnp.sum(jnp.where(lo, 0.0, sq), axis=-1, keepdims=True)
        scale = lax.rsqrt(jnp.where(lo, s_lo, s_hi) * (1.0 / seg) + EPS)
    return x * scale * g


def _attn_pre_kernel(x_ref, g_ref, win_ref, qn_ref, kn_ref, *out_refs, seg, q_scale, tb, emit_qf, emit_vt):
    out_refs = list(out_refs)
    qb_ref, k_ref, v_ref = out_refs[:3]
    qf_ref = out_refs[3] if emit_qf else None
    kb_ref, vt_ref = out_refs[-2:] if emit_vt else (None, None)
    d = x_ref.shape[1]
    h = _rms_norm(x_ref[...], g_ref[...]).astype(BF16)
    y = _dot(h, win_ref[...])
    tm = y.shape[0]
    for hd in range(ATTN_HEADS):
        sl = slice(hd * HEAD_W, (hd + 1) * HEAD_W)
        qh = _seg_norm(y[:, hd * HEAD_W:(hd + 1) * HEAD_W], qn_ref[:, sl], seg)
        kh = _seg_norm(y[:, d + hd * HEAD_W:d + (hd + 1) * HEAD_W], kn_ref[:, sl], seg)
        vh = y[:, 2 * d + hd * HEAD_W:2 * d + (hd + 1) * HEAD_W]
        qb_ref[:, sl] = (qh * q_scale).astype(BF16)
        k_ref[:, sl] = kh
        v_ref[:, sl] = vh
        if emit_qf:
            qf_ref[:, sl] = qh
        if emit_vt:
            kb_ref[:, sl] = kh.astype(BF16)
            for j in range(tm // tb):
                vt_ref[0, hd, j] = vh[j * tb:(j + 1) * tb].T.astype(BF16)


def _attn_pre(x, g, w_in, q_norm, k_norm, *, seg, q_scale, emit_qf=False, batch=None, tb=None, tm_pref=512):
    n, d = x.shape
    emit_vt = batch is not None
    tm = _row_tile(n, tm_pref)
    row = lambda i: (i, 0)
    full = lambda i: (0, 0)
    reps = d // q_norm.shape[0]
    dtypes = [BF16, F32, F32] + ([F32] if emit_qf else []) + ([BF16] if emit_vt else [])
    out_specs = [pl.BlockSpec((tm, d), row) for _ in dtypes]
    out_shape = [jax.ShapeDtypeStruct((n, d), dt) for dt in dtypes]
    if emit_vt:
        t = n // batch
        assert tm % tb == 0 and t % tm == 0
        tiles = t // tm
        out_specs.append(pl.BlockSpec((1, ATTN_HEADS, tm // tb, HEAD_W, tb),
                                      lambda i: (i // tiles, 0, i % tiles, 0, 0)))
        out_shape.append(jax.ShapeDtypeStruct((batch, ATTN_HEADS, t // tb, HEAD_W, tb), BF16))
    return pl.pallas_call(
        functools.partial(_attn_pre_kernel, seg=seg, q_scale=q_scale, tb=tb, emit_qf=emit_qf, emit_vt=emit_vt),
        grid=(n // tm,),
        in_specs=[pl.BlockSpec((tm, d), row), pl.BlockSpec((1, d), full),
                  pl.BlockSpec(w_in.shape, full), pl.BlockSpec((1, d), full), pl.BlockSpec((1, d), full)],
        out_specs=out_specs,
        out_shape=out_shape,
        compiler_params=_cparams(("parallel",), 56),
        name="attn_pre",
    )(x, g.reshape(1, d), w_in.astype(BF16),
      jnp.tile(q_norm, reps).reshape(1, d), jnp.tile(k_norm, reps).reshape(1, d))


def _bias_tile_kernel(bucket_ref, rb_ref, o_ref):
    h = pl.program_id(0)
    bucket = bucket_ref[...]
    far = rb_ref[FAR_BUCKET, h]
    acc = jnp.full(bucket.shape, NEG_INF, F32)
    for bk in range(REL_BUCKETS):
        acc = jnp.where(bucket == bk, (rb_ref[bk, h] - far) * LOG2E, acc)
    o_ref[0] = acc


def _bias_tiles(rel_bias, buckets):
    return pl.pallas_call(
        _bias_tile_kernel,
        grid=(ATTN_HEADS,),
        in_specs=[pl.BlockSpec(buckets.shape, lambda h: (0, 0, 0)),
                  pl.BlockSpec(memory_space=pltpu.SMEM)],
        out_specs=pl.BlockSpec((1,) + buckets.shape, lambda h: (h, 0, 0, 0)),
        out_shape=jax.ShapeDtypeStruct((ATTN_HEADS,) + buckets.shape, F32),
        compiler_params=_cparams(("parallel",), 32),
        name="bias_tiles",
    )(jnp.asarray(buckets), rel_bias)


def _block_buckets(tb):
    j = np.arange(tb)[:, None]
    i = np.arange(tb)[None, :]
    diag = np.where(i >= j, _rel_bucket_np(i - j), -1)
    prev = _rel_bucket_np(tb + i - j)
    return np.stack([diag, prev]).astype(np.int32)


def _softmax_group(s_list, vt_list, m_ref, l_ref, acc_ref, col_max=None):
    m_prev = m_ref[...]
    m_new = m_prev
    for i, s in enumerate(s_list):
        m_new = jnp.maximum(m_new, jnp.max(s, axis=0, keepdims=True) if col_max is None else col_max[i])
    alpha = jnp.exp2(m_prev - m_new)
    l_new = alpha * l_ref[...]
    acc = alpha * acc_ref[...]
    for s, vt in zip(s_list, vt_list):
        p = jnp.exp2(s - m_new)
        l_new = l_new + jnp.sum(p, axis=0, keepdims=True)
        acc = acc + _dot(vt, p.astype(BF16))
    m_ref[...] = m_new
    l_ref[...] = l_new
    acc_ref[...] = acc


def _drain_groups(n_groups, qk, softmax, buf_a, buf_b):
    def pair(j, carry):
        g = 2 * j
        qk(g + 1, buf_b)
        softmax(g, buf_a)
        qk(g + 2, buf_a)
        softmax(g + 1, buf_b)
        return carry

    lax.fori_loop(0, jnp.maximum(n_groups - 1, 0) // 2, pair, 0)
    last = n_groups - 1
    odd = jnp.bitwise_and(n_groups, 1) == 1

    @pl.when(odd)
    def _():
        softmax(last, buf_a)

    @pl.when(jnp.logical_and(n_groups > 0, jnp.logical_not(odd)))
    def _():
        qk(last, buf_b)
        softmax(last - 1, buf_a)
        softmax(last, buf_b)


def _diff_attn_kernel(q_ref, k_ref, vt_ref, bias_ref, lq1_ref, lk1_ref, lq2_ref, lk2_ref, subln_ref,
                      o_ref, s_a, s_b, mx_a, mx_b, m1, l1, a1, m2, l2, a2):
    tb = q_ref.shape[1]
    qi = pl.program_id(2)
    nq = pl.num_programs(2)
    q = q_ref[0]
    lane = lax.broadcasted_iota(jnp.int32, q.shape, 1)
    q1 = jnp.where(lane < DIFF_DK, q, jnp.zeros_like(q))
    q2 = jnp.where(lane < DIFF_DK, jnp.zeros_like(q), q)

    for m_ref, l_ref, a_ref in ((m1, l1, a1), (m2, l2, a2)):
        m_ref[...] = jnp.full(m_ref.shape, NEG_INF, F32)
        l_ref[...] = jnp.zeros(l_ref.shape, F32)
        a_ref[...] = jnp.zeros(a_ref.shape, F32)

    def scores(kb):
        kblk = k_ref[0, pl.ds(pl.multiple_of(kb * tb, tb), tb), :]
        return _dot_nt(kblk, q1), _dot_nt(kblk, q2)

    def update(blocks, s_pairs, max_pairs=None):
        vts = [vt_ref[0, 0, kb] for kb in blocks]
        for mp, (m_ref, l_ref, a_ref) in enumerate(((m1, l1, a1), (m2, l2, a2))):
            _softmax_group([s[mp] for s in s_pairs], vts, m_ref, l_ref, a_ref,
                           None if max_pairs is None else [mx[mp] for mx in max_pairs])

    n_far = jnp.maximum(qi - 1, 0)
    n_groups = n_far // DIFF_FAR_GROUP
    prev = jnp.maximum(qi - 1, 0)
    assert DIFF_FAR_GROUP == 2

    def group_blocks(g):
        far = [jnp.clip((g - 1) * DIFF_FAR_GROUP + i, 0, nq - 1) for i in range(DIFF_FAR_GROUP)]
        return [jnp.where(g == 0, near, f) for near, f in zip((qi, prev), far)]

    def store_scores(bufs, i, s_pair):
        buf, mx = bufs
        for mp, s in enumerate(s_pair):
            buf[i, mp] = s
            mx[i, mp] = jnp.max(s, axis=0, keepdims=True)

    def qk_group(g, bufs):
        for i, kb in enumerate(group_blocks(g)):
            store_scores(bufs, i, scores(kb))

    def softmax_group(g, bufs):
        buf, mx = bufs
        update(group_blocks(g), [(buf[i, 0], buf[i, 1]) for i in range(DIFF_FAR_GROUP)],
               [(mx[i, 0], mx[i, 1]) for i in range(DIFF_FAR_GROUP)])

    s_a, s_b = (s_a, mx_a), (s_b, mx_b)
    prev_bias = jnp.where(qi > 0, bias_ref[0, 1], NEG_INF)
    for i, (kb, bias) in enumerate(((qi, bias_ref[0, 0]), (prev, prev_bias))):
        store_scores(s_a, i, [s + bias for s in scores(kb)])
    _drain_groups(1 + n_groups, qk_group, softmax_group, s_a, s_b)

    for i in range(DIFF_FAR_GROUP - 1):
        kb = n_groups * DIFF_FAR_GROUP + i

        @pl.when(kb < n_far)
        def _(kb=kb):
            update([kb], [scores(kb)])

    lam = _lambda_full(lq1_ref, lk1_ref, lq2_ref, lk2_ref)
    o_t = a1[...] * (1.0 / l1[...]) - lam * (a2[...] * (1.0 / l2[...]))
    o = _rms_norm(o_t.T, subln_ref[...]) * (1.0 - DIFF_LAMBDA_INIT)
    o_ref[0] = o.astype(BF16)


def _diff_attn(qb, kb, vt, bias, lq1, lk1, lq2, lk2, subln):
    bsz, t, d = qb.shape
    tb = vt.shape[-1]
    nq = t // tb
    vec = lambda a: a.reshape(1, -1)
    vspec = lambda w: pl.BlockSpec((1, w), lambda b, h, i: (0, 0))
    stat = pltpu.VMEM((1, tb), F32)
    acc = pltpu.VMEM((HEAD_W, tb), F32)
    s_buf = pltpu.VMEM((DIFF_FAR_GROUP, 2, tb, tb), F32)
    mx_buf = pltpu.VMEM((DIFF_FAR_GROUP, 2, 1, tb), F32)
    return pl.pallas_call(
        _diff_attn_kernel,
        grid=(bsz, ATTN_HEADS, nq),
        in_specs=[pl.BlockSpec((1, tb, HEAD_W), lambda b, h, i: (b, i, h)),
                  pl.BlockSpec((1, t, HEAD_W), lambda b, h, i: (b, 0, h)),
                  pl.BlockSpec((1, 1, nq, HEAD_W, tb), lambda b, h, i: (b, h, 0, 0, 0)),
                  pl.BlockSpec((1, 2, tb, tb), lambda b, h, i: (h, 0, 0, 0)),
                  vspec(DIFF_DK), vspec(DIFF_DK), vspec(DIFF_DK), vspec(DIFF_DK), vspec(HEAD_W)],
        out_specs=pl.BlockSpec((1, tb, HEAD_W), lambda b, h, i: (b, i, h)),
        out_shape=jax.ShapeDtypeStruct((bsz, t, d), BF16),
        scratch_shapes=[s_buf, s_buf, mx_buf, mx_buf, stat, stat, acc, stat, stat, acc],
        compiler_params=_cparams(("parallel", "parallel", "arbitrary"), 56),
        name="diff_attn",
    )(qb, kb, vt, bias, vec(lq1), vec(lk1), vec(lq2), vec(lk2), vec(subln))


def _diff_decode_kernel(pt_ref, q_ref, ks_ref, vs_ref, blast_ref, bself_ref,
                        lq1_ref, lk1_ref, lq2_ref, lk2_ref, subln_ref, *rest):
    g = DEC_PAGES_PER_STEP
    k_pages = rest[:g]
    v_pages = rest[g:2 * g]
    o_ref, m1, l1, a1, m2, l2, a2 = rest[2 * g:]
    step = pl.program_id(1)
    last = pl.num_programs(1) - 1

    q = q_ref[0]
    lane = lax.broadcasted_iota(jnp.int32, q.shape, 1)
    q_maps = (jnp.where(lane < DIFF_DK, q, 0.0), jnp.where(lane < DIFF_DK, 0.0, q))
    states = ((m1, l1, a1), (m2, l2, a2))
    rr = lax.broadcasted_iota(jnp.int32, (2 * HEAD_W, 2 * HEAD_W), 0)
    cc = lax.broadcasted_iota(jnp.int32, (2 * HEAD_W, 2 * HEAD_W), 1)
    lane_sum = ((rr < HEAD_W) == (cc < HEAD_W)).astype(BF16)

    @pl.when(step == 0)
    def _():
        for qm, (m_ref, l_ref, a_ref) in zip(q_maps, states):
            s_self = jnp.sum(qm * ks_ref[0], axis=-1, keepdims=True) + bself_ref[...]
            m_ref[...] = jnp.broadcast_to(s_self, m_ref.shape)
            l_ref[...] = jnp.ones(l_ref.shape, F32)
            a_ref[...] = vs_ref[0]

    for i in range(g):
        k3 = k_pages[i][0]
        v3 = v_pages[i][0]
        prod = jnp.concatenate([k3 * q_maps[0], k3 * q_maps[1]], axis=-1)
        sums = _dot(prod.reshape(PAGE_SIZE * ATTN_HEADS, 2 * HEAD_W).astype(BF16), lane_sum)
        sums = sums.reshape(PAGE_SIZE, ATTN_HEADS, 2 * HEAD_W)
        for mp, (m_ref, l_ref, a_ref) in enumerate(states):
            s = sums[:, :, mp * HEAD_W:(mp + 1) * HEAD_W]
            if i == g - 1:
                s = s + jnp.where(step == last, blast_ref[...], 0.0)
            m_prev = m_ref[...]
            m_new = jnp.maximum(m_prev, jnp.max(s, axis=0))
            alpha = jnp.exp2(m_prev - m_new)
            p = jnp.exp2(s - m_new)
            l_ref[...] = alpha * l_ref[...] + jnp.sum(p, axis=0)
            a_ref[...] = alpha * a_ref[...] + jnp.sum(p * v3, axis=0)
            m_ref[...] = m_new

    @pl.when(step == last)
    def _():
        lam = _lambda_full(lq1_ref, lk1_ref, lq2_ref, lk2_ref)
        o = a1[...] * (1.0 / l1[...]) - lam * (a2[...] * (1.0 / l2[...]))
        o_ref[0] = (_rms_norm(o, subln_ref[...]) * (1.0 - DIFF_LAMBDA_INIT)).astype(BF16)


def _decode_bias(rel_bias):
    far = rel_bias[FAR_BUCKET]
    newest = (rel_bias[_rel_bucket_np(PAGE_SIZE - np.arange(PAGE_SIZE))] - far) * LOG2E
    own = (rel_bias[0] - far) * LOG2E
    return newest[:, :, None], own[:, None]


def _diff_decode(page_table, q, k_self, v_self, cache_k, cache_v, rel_bias, lq1, lk1, lq2, lk2, subln):
    n = q.shape[0]
    n_pages = page_table.shape[1]
    g = DEC_PAGES_PER_STEP
    assert n_pages % g == 0
    b_last, b_self = _decode_bias(rel_bias)
    per_head = lambda a: a.reshape(n, ATTN_HEADS, HEAD_W)
    vec = lambda a: a.reshape(1, -1)
    const = lambda shape: pl.BlockSpec(shape, lambda b, j, pt: (0,) * len(shape))
    seq = pl.BlockSpec((1, ATTN_HEADS, HEAD_W), lambda b, j, pt: (b, 0, 0))

    def page_spec(i):
        return pl.BlockSpec((1, PAGE_SIZE, ATTN_HEADS, HEAD_W),
                            lambda b, j, pt: (pt[b * n_pages + j * g + i], 0, 0, 0))

    stat = pltpu.VMEM((ATTN_HEADS, HEAD_W), F32)
    acc = pltpu.VMEM((ATTN_HEADS, HEAD_W), F32)
    out = pl.pallas_call(
        _diff_decode_kernel,
        grid_spec=pltpu.PrefetchScalarGridSpec(
            num_scalar_prefetch=1,
            grid=(n, n_pages // g),
            in_specs=[seq, seq, seq, const((PAGE_SIZE, ATTN_HEADS, 1)), const((ATTN_HEADS, 1)),
                      const((1, DIFF_DK)), const((1, DIFF_DK)), const((1, DIFF_DK)), const((1, DIFF_DK)),
                      const((1, HEAD_W))]
                     + [page_spec(i) for i in range(g)] + [page_spec(i) for i in range(g)],
            out_specs=seq,
            scratch_shapes=[stat, stat, acc, stat, stat, acc]),
        out_shape=jax.ShapeDtypeStruct((n, ATTN_HEADS, HEAD_W), BF16),
        compiler_params=_cparams(("parallel", "arbitrary"), 48),
        name="diff_decode",
    )(page_table.reshape(-1), per_head(q), per_head(k_self), per_head(v_self), b_last, b_self,
      vec(lq1), vec(lk1), vec(lq2), vec(lk2), vec(subln),
      *([cache_k] * g), *([cache_v] * g))
    return out.reshape(n, ATTN_HEADS * HEAD_W)


def _block_mean_kernel(k_ref, o_ref):
    for i in range(o_ref.shape[0]):
        o_ref[i:i + 1, :] = jnp.mean(k_ref[i * MOBA_BLOCK:(i + 1) * MOBA_BLOCK, :], axis=0, keepdims=True)


def _block_mean(k):
    n, d = k.shape
    nb = n // MOBA_BLOCK
    per = _row_tile(nb, 8)
    return pl.pallas_call(
        _block_mean_kernel,
        grid=(nb // per,),
        in_specs=[pl.BlockSpec((per * MOBA_BLOCK, d), lambda i: (i, 0))],
        out_specs=pl.BlockSpec((per, d), lambda i: (i, 0)),
        out_shape=jax.ShapeDtypeStruct((nb, d), F32),
        compiler_params=_cparams(("parallel",), 32),
        name="moba_block_mean",
    )(k)


def _topk_rows(gate, row_idx, k):
    picks = []
    for _ in range(k):
        best = jnp.max(gate, axis=0, keepdims=True)
        first = jnp.min(jnp.where(gate == best, row_idx, gate.shape[0]), axis=0, keepdims=True)
        picks.append(first)
        gate = jnp.where(row_idx == first, NEG_INF, gate)
    return picks


def _moba_attn_kernel(qf_ref, qb_ref, k_ref, vt_ref, kmean_ref, bias_ref, o_ref,
                      sel_scr, s_a, s_b, mx_a, mx_b, m_scr, l_scr, acc_scr):
    tb = qf_ref.shape[1]
    qi = pl.program_id(2)
    q = qb_ref[0]

    gate = _dot_nt(kmean_ref[0], qf_ref[0], precision=HIGHEST)
    blk_idx = lax.broadcasted_iota(jnp.int32, gate.shape, 0)
    past = blk_idx < qi
    picks = _topk_rows(jnp.where(past, gate, NEG_INF), blk_idx, MOBA_TOPK)
    routed = functools.reduce(jnp.logical_or, [blk_idx == p for p in picks])
    sel_scr[...] = (past & routed).astype(F32)

    m_scr[...] = jnp.full(m_scr.shape, NEG_INF, F32)
    l_scr[...] = jnp.zeros(l_scr.shape, F32)
    acc_scr[...] = jnp.zeros(acc_scr.shape, F32)

    nq = pl.num_programs(2)

    def scores(kb, bias, keep):
        kblk = k_ref[0, pl.ds(pl.multiple_of(kb * tb, tb), tb), :]
        s = _dot_nt(kblk, q)
        if bias is not None:
            s = s + bias
        if keep is not None:
            s = jnp.where(keep > 0.0, s, NEG_INF)
        return s

    near = MOBA_FAR_GROUP
    n_far = jnp.maximum(qi - (near - 1), 0)
    n_groups = (n_far + MOBA_FAR_GROUP - 1) // MOBA_FAR_GROUP

    def group_blocks(g):
        out = []
        for i in range(MOBA_FAR_GROUP):
            kb_far = (g - 1) * MOBA_FAR_GROUP + i
            kb = jnp.where(g == 0, qi - i, kb_far)
            valid = jnp.where(g == 0, qi - i >= 0, kb_far < n_far)
            out.append((jnp.clip(kb, 0, nq - 1), valid))
        return out

    def routed(kb, valid):
        return jnp.where(valid, sel_scr[pl.ds(kb, 1), :], 0.0)

    def store_scores(bufs, i, s):
        buf, mx = bufs
        buf[i] = s
        mx[i] = jnp.max(s, axis=0, keepdims=True)

    def qk_group(g, bufs):
        for i, (kb, valid) in enumerate(group_blocks(g)):
            store_scores(bufs, i, scores(kb, None, routed(kb, valid)))

    def softmax_group(g, bufs):
        buf, mx = bufs
        _softmax_group([buf[i] for i in range(MOBA_FAR_GROUP)], [vt_ref[0, 0, kb] for kb, _ in group_blocks(g)],
                       m_scr, l_scr, acc_scr, [mx[i] for i in range(MOBA_FAR_GROUP)])

    s_a, s_b = (s_a, mx_a), (s_b, mx_b)
    for i, (kb, valid) in enumerate(group_blocks(0)):
        bias = bias_ref[0, i] if i < 2 else None
        store_scores(s_a, i, scores(kb, bias, None if i == 0 else routed(kb, valid)))
    _drain_groups(1 + n_groups, qk_group, softmax_group, s_a, s_b)
    o_ref[0] = (acc_scr[...] * (1.0 / l_scr[...])).T.astype(BF16)


def _moba_attn(qf, qb, kb, vt, kmean, bias):
    bsz, t, d = qb.shape
    tb = vt.shape[-1]
    assert tb == MOBA_BLOCK
    nq = t // tb
    nb = kmean.shape[1]
    qspec = pl.BlockSpec((1, tb, HEAD_W), lambda b, h, i: (b, i, h))
    s_buf = pltpu.VMEM((MOBA_FAR_GROUP, tb, tb), F32)
    mx_buf = pltpu.VMEM((MOBA_FAR_GROUP, 1, tb), F32)
    return pl.pallas_call(
        _moba_attn_kernel,
        grid=(bsz, ATTN_HEADS, nq),
        in_specs=[qspec, qspec,
                  pl.BlockSpec((1, t, HEAD_W), lambda b, h, i: (b, 0, h)),
                  pl.BlockSpec((1, 1, nq, HEAD_W, tb), lambda b, h, i: (b, h, 0, 0, 0)),
                  pl.BlockSpec((1, nb, HEAD_W), lambda b, h, i: (b, 0, h)),
                  pl.BlockSpec((1, 2, tb, tb), lambda b, h, i: (h, 0, 0, 0))],
        out_specs=qspec,
        out_shape=jax.ShapeDtypeStruct((bsz, t, d), BF16),
        scratch_shapes=[pltpu.VMEM((nb, tb), F32), s_buf, s_buf, mx_buf, mx_buf,
                        pltpu.VMEM((1, tb), F32), pltpu.VMEM((1, tb), F32),
                        pltpu.VMEM((HEAD_W, tb), F32)],
        compiler_params=_cparams(("parallel", "parallel", "arbitrary"), 32),
        name="moba_attn",
    )(qf, qb, kb, vt, kmean, bias)


def _moba_route_kernel(pt_ref, qf_ref, *rest):
    g = DEC_PAGES_PER_STEP
    k_pages = rest[:g]
    sel_ref, kmean_scr = rest[g:]
    step = pl.program_id(1)
    pages_per_block = MOBA_BLOCK // PAGE_SIZE
    blocks_per_step = g // pages_per_block
    for i in range(blocks_per_step):
        tot = jnp.sum(k_pages[i * pages_per_block][0], axis=0)
        for p in range(1, pages_per_block):
            tot = tot + jnp.sum(k_pages[i * pages_per_block + p][0], axis=0)
        kmean_scr[step * blocks_per_step + i] = tot * (1.0 / MOBA_BLOCK)

    @pl.when(step == pl.num_programs(1) - 1)
    def _():
        gate = jnp.sum(kmean_scr[...] * qf_ref[0], axis=-1, keepdims=True)
        blk_idx = lax.broadcasted_iota(jnp.int32, gate.shape, 0)
        for t, chosen in enumerate(_topk_rows(gate, blk_idx, MOBA_TOPK)):
            sel_ref[0, t] = jnp.broadcast_to(chosen[0], sel_ref.shape[2:])


def _moba_route(page_table, qf, cache_k):
    n = qf.shape[0]
    n_pages = page_table.shape[1]
    g = DEC_PAGES_PER_STEP
    nb = n_pages * PAGE_SIZE // MOBA_BLOCK
    assert n_pages % g == 0 and g % (MOBA_BLOCK // PAGE_SIZE) == 0 and nb >= MOBA_TOPK

    def page_spec(i):
        return pl.BlockSpec((1, PAGE_SIZE, ATTN_HEADS, HEAD_W),
                            lambda b, j, pt: (pt[b * n_pages + j * g + i], 0, 0, 0))

    return pl.pallas_call(
        _moba_route_kernel,
        grid_spec=pltpu.PrefetchScalarGridSpec(
            num_scalar_prefetch=1,
            grid=(n, n_pages // g),
            in_specs=[pl.BlockSpec((1, ATTN_HEADS, HEAD_W), lambda b, j, pt: (b, 0, 0))]
                     + [page_spec(i) for i in range(g)],
            out_specs=pl.BlockSpec((1, MOBA_TOPK, ATTN_HEADS, HEAD_W), lambda b, j, pt: (b, 0, 0, 0)),
            scratch_shapes=[pltpu.VMEM((nb, ATTN_HEADS, HEAD_W), F32)]),
        out_shape=jax.ShapeDtypeStruct((n, MOBA_TOPK, ATTN_HEADS, HEAD_W), jnp.int32),
        compiler_params=_cparams(("parallel", "arbitrary"), 32),
        name="moba_route",
    )(page_table.reshape(-1), qf.reshape(n, ATTN_HEADS, HEAD_W), *([cache_k] * g))


def _moba_decode_kernel(pt_ref, sel_ref, q_ref, ks_ref, vs_ref, bias_ref, ck_ref, cv_ref, o_ref,
                        k_buf, v_buf, sems, *, n_pages, newest_block):
    pages_per_block = MOBA_BLOCK // PAGE_SIZE
    n_sel = MOBA_TOPK * pages_per_block
    b = pl.program_id(0)

    def block_of(h, t):
        return sel_ref[(b * MOBA_TOPK + t) * ATTN_HEADS + h]

    def copies(h, i):
        page = pt_ref[b * n_pages + block_of(h, i // pages_per_block) * pages_per_block + i % pages_per_block]
        return (pltpu.make_async_copy(ck_ref.at[page, :, h, :], k_buf.at[h, i], sems.at[0, h, i]),
                pltpu.make_async_copy(cv_ref.at[page, :, h, :], v_buf.at[h, i], sems.at[1, h, i]))

    for h in range(ATTN_HEADS):
        for i in range(n_sel):
            for cp in copies(h, i):
                cp.start()

    for h in range(ATTN_HEADS):
        for i in range(n_sel):
            for cp in copies(h, i):
                cp.wait()

    for h in range(ATTN_HEADS):
        q = q_ref[0, h:h + 1, :]
        q8 = jnp.broadcast_to(q, (8, HEAD_W)).astype(BF16)
        s_self = jnp.sum(q * ks_ref[0, h:h + 1, :], axis=-1, keepdims=True) + bias_ref[h, 1:2, 0:1]
        k_all = k_buf[h].reshape(n_sel * PAGE_SIZE, HEAD_W).astype(BF16)
        v_all = v_buf[h].reshape(n_sel * PAGE_SIZE, HEAD_W).astype(BF16)
        near = [jnp.where((i % pages_per_block == pages_per_block - 1) & (block_of(h, i // pages_per_block) == newest_block),
                          bias_ref[h, 0:1, :], 0.0) for i in range(n_sel)]
        s = _dot_nt(q8, k_all)[0:1] + jnp.concatenate(near, axis=-1)
        m = jnp.maximum(s_self, jnp.max(s, axis=-1, keepdims=True))
        p_self = jnp.exp2(s_self - m)
        p = jnp.exp2(s - m)
        l = p_self + jnp.sum(p, axis=-1, keepdims=True)
        o = p_self * vs_ref[0, h:h + 1, :] + _dot(jnp.broadcast_to(p, (8, p.shape[1])).astype(BF16), v_all)[0:1]
        o_ref[0, h:h + 1, :] = (o * (1.0 / l)).astype(BF16)


def _moba_decode(page_table, sel, q, k_self, v_self, cache_k, cache_v, rel_bias):
    n = q.shape[0]
    n_pages = page_table.shape[1]
    pages_per_block = MOBA_BLOCK // PAGE_SIZE
    n_sel = MOBA_TOPK * pages_per_block
    b_last, b_self = _decode_bias(rel_bias)
    bias = jnp.stack([b_last[:, :, 0].T, jnp.broadcast_to(b_self, (ATTN_HEADS, PAGE_SIZE))], axis=1)
    per_head = lambda a: a.reshape(n, ATTN_HEADS, HEAD_W)
    seq = pl.BlockSpec((1, ATTN_HEADS, HEAD_W), lambda b, pt, sl: (b, 0, 0))
    page_buf = pltpu.VMEM((ATTN_HEADS, n_sel, PAGE_SIZE, HEAD_W), F32)
    out = pl.pallas_call(
        functools.partial(_moba_decode_kernel, n_pages=n_pages, newest_block=n_pages // pages_per_block - 1),
        grid_spec=pltpu.PrefetchScalarGridSpec(
            num_scalar_prefetch=2,
            grid=(n,),
            in_specs=[seq, seq, seq, pl.BlockSpec(bias.shape, lambda b, pt, sl: (0, 0, 0)),
                      pl.BlockSpec(memory_space=pl.ANY), pl.BlockSpec(memory_space=pl.ANY)],
            out_specs=seq,
            scratch_shapes=[page_buf, page_buf, pltpu.SemaphoreType.DMA((2, ATTN_HEADS, n_sel))]),
        out_shape=jax.ShapeDtypeStruct((n, ATTN_HEADS, HEAD_W), BF16),
        compiler_params=_cparams(("arbitrary",), 32),
        name="moba_decode",
    )(page_table.reshape(-1), sel[:, :, :, 0].reshape(-1),
      per_head(q), per_head(k_self), per_head(v_self), bias, cache_k, cache_v)
    return out.reshape(n, ATTN_HEADS * HEAD_W)


def kernel(x_prompt, x_sample, state_gla_l0, cache_k_l1, cache_v_l1, cache_k_l2, cache_v_l2, state_gla_l3, page_table, rel_bias, norm_mix, norm_mlp, mlp_w_up, mlp_w_down, gla_w_in, gla_w_a1, gla_w_a2, gla_b_a, gla_norm, gla_w_out, diff_w_in, diff_q_norm, diff_k_norm, diff_lq1, diff_lk1, diff_lq2, diff_lk2, diff_subln, diff_w_out, moba_w_in, moba_q_norm, moba_k_norm, moba_w_out):
    bsz, t, d = x_prompt.shape
    n_dec = x_sample.shape[0]
    assert d == D_MODEL and x_sample.shape[1] == 1
    xp = x_prompt.reshape(bsz * t, d)
    xs = x_sample.reshape(n_dec, d)
    seq = lambda a: a.reshape(bsz, t, a.shape[-1])
    heads = lambda a: a.reshape(a.shape[:-1] + (ATTN_HEADS, HEAD_W))

    def mlp(x, mo, w_out, li):
        return _post(x, mo, w_out, norm_mlp[li], mlp_w_up[li], mlp_w_down[li])

    def gla_layer(xp, xs, li, gi, state_dec):
        w = (gla_w_in[gi], gla_w_a1[gi], gla_w_a2[gi], gla_b_a[gi])
        q, k, v, r, la = _gla_pre(xp, norm_mix[li], *w)
        s0 = jnp.zeros((bsz,) + state_dec.shape[1:], F32)
        mo_p, st_p = _gla_scan(seq(q), seq(k), seq(la), seq(v), seq(r), gla_norm[gi], s0)
        q, k, v, r, la = _gla_pre(xs, norm_mix[li], *w)
        mo_s, st_s = _gla_step(q, k, la, v, r, gla_norm[gi], state_dec)
        xp = mlp(xp, mo_p.reshape(bsz * t, d), gla_w_out[gi], li)
        xs = mlp(xs, mo_s, gla_w_out[gi], li)
        return xp, xs, st_p, st_s

    xp, xs, st_p0, st_s0 = gla_layer(xp, xs, 0, 0, state_gla_l0)

    tb = min(DIFF_TB, t)
    lam_w = (diff_lq1, diff_lk1, diff_lq2, diff_lk2, diff_subln)
    pre = dict(seg=DIFF_DK, q_scale=DIFF_DK ** -0.5 * LOG2E)
    qb, k, v, kb, vt = _attn_pre(xp, norm_mix[1], diff_w_in, diff_q_norm, diff_k_norm, batch=bsz, tb=tb, **pre)
    bias = _bias_tiles(rel_bias, _block_buckets(tb))
    mo_p = _diff_attn(seq(qb), seq(kb), vt, bias, *lam_w)
    k_p1, v_p1 = heads(seq(k)), heads(seq(v))
    qb, k, v = _attn_pre(xs, norm_mix[1], diff_w_in, diff_q_norm, diff_k_norm, **pre)
    mo_s = _diff_decode(page_table, qb.astype(F32), k, v, cache_k_l1, cache_v_l1, rel_bias, *lam_w)
    k_s1, v_s1 = heads(k.reshape(n_dec, 1, d)), heads(v.reshape(n_dec, 1, d))
    xp = mlp(xp, mo_p.reshape(bsz * t, d), diff_w_out, 1)
    xs = mlp(xs, mo_s, diff_w_out, 1)

    pre = dict(seg=HEAD_W, q_scale=HEAD_W ** -0.5 * LOG2E, emit_qf=True)
    qb, k, v, qf, kb, vt = _attn_pre(xp, norm_mix[2], moba_w_in, moba_q_norm, moba_k_norm, batch=bsz, tb=MOBA_TB, **pre)
    kmean = _block_mean(k).reshape(bsz, t // MOBA_BLOCK, d)
    bias = _bias_tiles(rel_bias, _block_buckets(MOBA_TB))
    mo_p = _moba_attn(seq(qf), seq(qb), seq(kb), vt, kmean, bias)
    k_p2, v_p2 = heads(seq(k)), heads(seq(v))
    qb, k, v, qf = _attn_pre(xs, norm_mix[2], moba_w_in, moba_q_norm, moba_k_norm, **pre)
    sel = _moba_route(page_table, qf, cache_k_l2)
    mo_s = _moba_decode(page_table, sel, qb.astype(F32), k, v, cache_k_l2, cache_v_l2, rel_bias)
    k_s2, v_s2 = heads(k.reshape(n_dec, 1, d)), heads(v.reshape(n_dec, 1, d))
    xp = mlp(xp, mo_p.reshape(bsz * t, d), moba_w_out, 2)
    xs = mlp(xs, mo_s, moba_w_out, 2)

    xp, xs, st_p3, st_s3 = gla_layer(xp, xs, 3, 1, state_gla_l3)

    return (xp.reshape(bsz, t, d), xs.reshape(n_dec, 1, d), st_p0, st_s0,
            k_p1, v_p1, k_s1, v_s1, k_p2, v_p2, k_s2, v_s2, st_p3, st_s3)
```

```python
import functools
import math

import numpy as np
import jax
import jax.numpy as jnp
from jax import lax
from jax.experimental import pallas as pl
from jax.experimental.pallas import tpu as pltpu

F32 = jnp.float32
BF16 = jnp.bfloat16
NEG_INF = float("-inf")
HIGHEST = lax.Precision.HIGHEST

EPS = 1e-6
D_MODEL = 1024
PAGE_SIZE = 128

GLA_HEADS = 4
GLA_DK = 128
GLA_DV = 256
GLA_QK = GLA_HEADS * GLA_DK
GLA_V = GLA_HEADS * GLA_DV
GLA_TAU = 16.0
GLA_RANK_PAD = 128
GLA_CHUNK = 128
GLA_SUB = 16

ATTN_HEADS = 8
HEAD_W = 128
DIFF_DK = 64
DIFF_LAMBDA_INIT = 0.8 - 0.6 * math.exp(-0.3 * 1)
MOBA_BLOCK = 256
MOBA_TOPK = 3
DIFF_TB = 512
MOBA_TB = MOBA_BLOCK
DIFF_FAR_GROUP = 2
MOBA_FAR_GROUP = 4
LOG2E = math.log2(math.e)

REL_BUCKETS = 32
REL_MAX_EXACT = 16
REL_MAX_DIST = 128
FAR_BUCKET = REL_BUCKETS - 1

V7X_VMEM_BYTES = 64 * 1024 * 1024
MIB = 1024 * 1024
DEC_PAGES_PER_STEP = 8
DEC_TOKEN_CHUNK = 32

NT_DIMS = (((1,), (1,)), ((), ()))


def _cparams(sem, vmem_mib):
    assert vmem_mib * MIB < V7X_VMEM_BYTES
    return pltpu.CompilerParams(dimension_semantics=sem, vmem_limit_bytes=vmem_mib * MIB)


def _rel_bucket_np(dist):
    dist = np.maximum(dist, 0)
    df = np.maximum(dist, 1).astype(np.float32)
    large = REL_MAX_EXACT + (np.log(df / np.float32(REL_MAX_EXACT)) / np.float32(math.log(REL_MAX_DIST / REL_MAX_EXACT))
                             * np.float32(REL_BUCKETS - REL_MAX_EXACT)).astype(np.int32)
    large = np.minimum(large, REL_BUCKETS - 1)
    return np.where(dist < REL_MAX_EXACT, dist, large).astype(np.int32)


def _rms_norm(x, g):
    return x * lax.rsqrt(jnp.mean(x * x, axis=-1, keepdims=True) + EPS) * g


def _log_sigmoid(x):
    return jnp.minimum(x, 0.0) - jnp.log1p(jnp.exp(-jnp.abs(x)))


def _silu(x):
    return x * (1.0 / (1.0 + jnp.exp(-x)))


def _dot(a, b):
    return jnp.dot(a, b, preferred_element_type=F32)


def _dot_nt(a, b, precision=None):
    return lax.dot_general(a, b, NT_DIMS, precision=precision, preferred_element_type=F32)


def _lambda_full(lq1_ref, lk1_ref, lq2_ref, lk2_ref):
    return (jnp.exp(jnp.sum(lq1_ref[...] * lk1_ref[...], axis=-1, keepdims=True))
            - jnp.exp(jnp.sum(lq2_ref[...] * lk2_ref[...], axis=-1, keepdims=True))
            + DIFF_LAMBDA_INIT)


def _row_tile(n, pref):
    t = min(n, pref)
    assert n % t == 0
    return t


def _gla_pre_kernel(x_ref, g_ref, win_ref, wa1_ref, wa2_ref, ba_ref,
                    q_ref, k_ref, v_ref, r_ref, la_ref):
    h = _rms_norm(x_ref[...], g_ref[...]).astype(BF16)
    y = _dot(h, win_ref[...])
    q_ref[...] = y[:, :GLA_QK] * (GLA_DK ** -0.5)
    k_ref[...] = y[:, GLA_QK:2 * GLA_QK]
    v_ref[...] = y[:, 2 * GLA_QK:2 * GLA_QK + GLA_V]
    r_ref[...] = y[:, 2 * GLA_QK + GLA_V:]
    low = _dot(h, wa1_ref[...]).astype(BF16)
    gate = _dot(low, wa2_ref[...]) + ba_ref[...]
    la_ref[...] = _log_sigmoid(gate) * (1.0 / GLA_TAU)


def _gla_pre(x, g, w_in, w_a1, w_a2, b_a, tm_pref=512):
    n, d = x.shape
    tm = _row_tile(n, tm_pref)
    rank = w_a1.shape[1]
    wa1 = jnp.pad(w_a1, ((0, 0), (0, GLA_RANK_PAD - rank))).astype(BF16)
    wa2 = jnp.pad(w_a2, ((0, GLA_RANK_PAD - rank), (0, 0))).astype(BF16)
    row = lambda i: (i, 0)
    full = lambda i: (0, 0)
    widths = (GLA_QK, GLA_QK, GLA_V, GLA_V, GLA_QK)
    return pl.pallas_call(
        _gla_pre_kernel,
        grid=(n // tm,),
        in_specs=[pl.BlockSpec((tm, d), row), pl.BlockSpec((1, d), full),
                  pl.BlockSpec(w_in.shape, full), pl.BlockSpec(wa1.shape, full),
                  pl.BlockSpec(wa2.shape, full), pl.BlockSpec((1, GLA_QK), full)],
        out_specs=[pl.BlockSpec((tm, w), row) for w in widths],
        out_shape=[jax.ShapeDtypeStruct((n, w), F32) for w in widths],
        compiler_params=_cparams(("parallel",), 48),
        name="gla_pre",
    )(x, g.reshape(1, d), w_in.astype(BF16), wa1, wa2, b_a.reshape(1, GLA_QK))


def _gla_scan_kernel(q_ref, k_ref, la_ref, v_ref, r_ref, gn_ref, s0_ref,
                     mo_ref, s_ref, b_scr):
    chunk = q_ref.shape[1]
    nsub = chunk // GLA_SUB

    @pl.when(pl.program_id(1) == 0)
    def _():
        s_ref[...] = s0_ref[...]

    ri = lax.broadcasted_iota(jnp.int32, (chunk, chunk), 0)
    ci = lax.broadcasted_iota(jnp.int32, (chunk, chunk), 1)
    tri = (ri >= ci).astype(F32)
    b_scr[...] = jnp.dot(tri, la_ref[0], precision=HIGHEST, preferred_element_type=F32)
    sub_row = lax.broadcasted_iota(jnp.int32, (GLA_SUB, GLA_DK), 0)
    sub_col = lax.broadcasted_iota(jnp.int32, (GLA_SUB, chunk), 1)
    gn = gn_ref[...]

    for h in range(GLA_HEADS):
        ks = slice(h * GLA_DK, (h + 1) * GLA_DK)
        vs = slice(h * GLA_DV, (h + 1) * GLA_DV)
        q = q_ref[0, :, ks]
        k = k_ref[0, :, ks]
        v_bf = v_ref[0, :, vs].astype(BF16)
        b = b_scr[:, ks]
        b_last = b[chunk - 1:chunk, :]

        s_old = s_ref[0, h]
        o_inter = _dot((q * jnp.exp(b)).astype(BF16), s_old.astype(BF16))

        k_dec_t = (k * jnp.exp(b_last - b)).T
        decay_col = jnp.broadcast_to(jnp.exp(b_last), (GLA_DK, GLA_DK)).T
        decay = jnp.concatenate([decay_col] * (GLA_DV // GLA_DK), axis=1)
        s_ref[0, h] = decay * s_old + _dot(k_dec_t.astype(BF16), v_bf)

        for blk in range(nsub):
            r0 = blk * GLA_SUB
            rows = slice(r0, r0 + GLA_SUB)
            q_i = q[rows]
            b_i = b[rows]
            acc = o_inter[rows]
            if blk > 0:
                ref_b = b_scr[r0 - 1:r0, ks]
                q_t = (q_i * jnp.exp(b_i - ref_b)).astype(BF16)
                k_t = (k[:r0] * jnp.exp(ref_b - b[:r0])).astype(BF16)
                a_off = _dot_nt(q_t, k_t)
                acc = acc + _dot(a_off.astype(BF16), v_bf[:r0])
            diag = jnp.zeros((GLA_SUB, chunk), F32)
            for j in range(GLA_SUB):
                k_j = k_ref[0, r0 + j:r0 + j + 1, ks]
                b_j = b_scr[r0 + j:r0 + j + 1, ks]
                e = jnp.where(sub_row >= j, b_i - b_j, NEG_INF)
                w = jnp.sum(q_i * k_j * jnp.exp(e), axis=-1, keepdims=True)
                diag = jnp.where(sub_col == r0 + j, w, diag)
            acc = acc + _dot(diag.astype(BF16), v_bf)
            o_n = _rms_norm(acc, gn) * _silu(r_ref[0, rows, vs])
            mo_ref[0, rows, vs] = o_n.astype(BF16)


def _gla_scan(q, k, la, v, r, g_norm, s0):
    bsz, t, _ = q.shape
    chunk = _row_tile(t, GLA_CHUNK)
    assert chunk % GLA_SUB == 0
    qk_spec = pl.BlockSpec((1, chunk, GLA_QK), lambda b, c: (b, c, 0))
    v_spec = pl.BlockSpec((1, chunk, GLA_V), lambda b, c: (b, c, 0))
    s_spec = pl.BlockSpec((1, GLA_HEADS, GLA_DK, GLA_DV), lambda b, c: (b, 0, 0, 0))
    return pl.pallas_call(
        _gla_scan_kernel,
        grid=(bsz, t // chunk),
        in_specs=[qk_spec, qk_spec, qk_spec, v_spec, v_spec,
                  pl.BlockSpec((1, GLA_DV), lambda b, c: (0, 0)), s_spec],
        out_specs=[v_spec, s_spec],
        out_shape=[jax.ShapeDtypeStruct((bsz, t, GLA_V), BF16),
                   jax.ShapeDtypeStruct((bsz, GLA_HEADS, GLA_DK, GLA_DV), F32)],
        scratch_shapes=[pltpu.VMEM((chunk, GLA_QK), F32)],
        compiler_params=_cparams(("parallel", "arbitrary"), 32),
        name="gla_scan",
    )(q, k, la, v, r, g_norm.reshape(1, GLA_DV), s0)


def _gla_step_kernel(q_ref, k_ref, la_ref, v_ref, r_ref, gn_ref, s0_ref, mo_ref, s_ref):
    gn = gn_ref[...]
    for h in range(GLA_HEADS):
        ks = slice(h * GLA_DK, (h + 1) * GLA_DK)
        vs = slice(h * GLA_DV, (h + 1) * GLA_DV)
        q = q_ref[0, :, ks]
        k = k_ref[0, :, ks]
        a = jnp.exp(la_ref[0, :, ks])
        v = v_ref[0, :, vs]
        s_old = s0_ref[0, h]

        def col(row_vec):
            sq = jnp.broadcast_to(row_vec, (GLA_DK, GLA_DK)).T
            return jnp.concatenate([sq] * (GLA_DV // GLA_DK), axis=1)

        s_ref[0, h] = col(a) * s_old + col(k) * v
        o = jnp.sum(col(q * a) * s_old, axis=0, keepdims=True) + jnp.sum(q * k, axis=-1, keepdims=True) * v
        o_n = _rms_norm(o, gn) * _silu(r_ref[0, :, vs])
        mo_ref[0, :, vs] = o_n.astype(BF16)


def _gla_step(q, k, la, v, r, g_norm, s0):
    n = q.shape[0]
    qk_spec = pl.BlockSpec((1, 1, GLA_QK), lambda b: (b, 0, 0))
    v_spec = pl.BlockSpec((1, 1, GLA_V), lambda b: (b, 0, 0))
    s_spec = pl.BlockSpec((1, GLA_HEADS, GLA_DK, GLA_DV), lambda b: (b, 0, 0, 0))
    r3 = lambda a: a.reshape(n, 1, a.shape[-1])
    mo, s_new = pl.pallas_call(
        _gla_step_kernel,
        grid=(n,),
        in_specs=[qk_spec, qk_spec, qk_spec, v_spec, v_spec,
                  pl.BlockSpec((1, GLA_DV), lambda b: (0, 0)), s_spec],
        out_specs=[v_spec, s_spec],
        out_shape=[jax.ShapeDtypeStruct((n, 1, GLA_V), BF16),
                   jax.ShapeDtypeStruct(s0.shape, F32)],
        compiler_params=_cparams(("parallel",), 16),
        name="gla_step",
    )(r3(q), r3(k), r3(la), r3(v), r3(r), g_norm.reshape(1, GLA_DV), s0)
    return mo.reshape(n, GLA_V), s_new


def _post_kernel(x_ref, mo_ref, wo_ref, g_ref, wup_ref, wdn_ref, y_ref, h_scr):
    @pl.when(pl.program_id(1) == 0)
    def _():
        x1 = x_ref[...] + _dot(mo_ref[...], wo_ref[...])
        y_ref[...] = x1
        h_scr[...] = _rms_norm(x1, g_ref[...]).astype(BF16)

    a = jnp.maximum(_dot(h_scr[...], wup_ref[...]), 0.0)
    y_ref[...] += _dot((a * a).astype(BF16), wdn_ref[...])


def _post(x, mo, w_out, g, w_up, w_down, tm_pref=512, tf_pref=2048):
    n, d = x.shape
    d_ff = w_up.shape[1]
    tm = _row_tile(n, tm_pref)
    tf = _row_tile(d_ff, tf_pref)
    return pl.pallas_call(
        _post_kernel,
        grid=(n // tm, d_ff // tf),
        in_specs=[pl.BlockSpec((tm, d), lambda i, j: (i, 0)),
                  pl.BlockSpec((tm, d), lambda i, j: (i, 0)),
                  pl.BlockSpec((d, d), lambda i, j: (0, 0)),
                  pl.BlockSpec((1, d), lambda i, j: (0, 0)),
                  pl.BlockSpec((d, tf), lambda i, j: (0, j)),
                  pl.BlockSpec((tf, d), lambda i, j: (j, 0))],
        out_specs=pl.BlockSpec((tm, d), lambda i, j: (i, 0)),
        out_shape=jax.ShapeDtypeStruct((n, d), F32),
        scratch_shapes=[pltpu.VMEM((tm, d), BF16)],
        compiler_params=_cparams(("parallel", "arbitrary"), 48),
        name="post_mlp",
    )(x, mo, w_out.astype(BF16), g.reshape(1, d), w_up.astype(BF16), w_down.astype(BF16))


def _seg_norm(x, g, seg):
    sq = x * x
    if seg == HEAD_W:
        scale = lax.rsqrt(jnp.mean(sq, axis=-1, keepdims=True) + EPS)
    else:
        assert 2 * seg == HEAD_W
        lo = lax.broadcasted_iota(jnp.int32, x.shape, 1) < seg
        s_lo = jnp.sum(jnp.where(lo, sq, 0.0), axis=-1, keepdims=True)
        s_hi = jnp.sum(jnp.where(lo, 0.0, sq), axis=-1, keepdims=True)
        scale = lax.rsqrt(jnp.where(lo, s_lo, s_hi) * (1.0 / seg) + EPS)
    return x * scale * g


def _attn_pre_kernel(x_ref, g_ref, win_ref, qn_ref, kn_ref, *out_refs, seg, q_scale, tb, emit_qf, emit_vt):
    out_refs = list(out_refs)
    qb_ref, k_ref, v_ref = out_refs[:3]
    qf_ref = out_refs[3] if emit_qf else None
    kb_ref, vt_ref = out_refs[-2:] if emit_vt else (None, None)
    d = x_ref.shape[1]
    h = _rms_norm(x_ref[...], g_ref[...]).astype(BF16)
    y = _dot(h, win_ref[...])
    tm = y.shape[0]
    for hd in range(ATTN_HEADS):
        sl = slice(hd * HEAD_W, (hd + 1) * HEAD_W)
        qh = _seg_norm(y[:, hd * HEAD_W:(hd + 1) * HEAD_W], qn_ref[:, sl], seg)
        kh = _seg_norm(y[:, d + hd * HEAD_W:d + (hd + 1) * HEAD_W], kn_ref[:, sl], seg)
        vh = y[:, 2 * d + hd * HEAD_W:2 * d + (hd + 1) * HEAD_W]
        qb_ref[:, sl] = (qh * q_scale).astype(BF16)
        k_ref[:, sl] = kh
        v_ref[:, sl] = vh
        if emit_qf:
            qf_ref[:, sl] = qh
        if emit_vt:
            kb_ref[:, sl] = kh.astype(BF16)
            for j in range(tm // tb):
                vt_ref[0, hd, j] = vh[j * tb:(j + 1) * tb].T.astype(BF16)


def _attn_pre(x, g, w_in, q_norm, k_norm, *, seg, q_scale, emit_qf=False, batch=None, tb=None, tm_pref=512):
    n, d = x.shape
    emit_vt = batch is not None
    tm = _row_tile(n, tm_pref)
    row = lambda i: (i, 0)
    full = lambda i: (0, 0)
    reps = d // q_norm.shape[0]
    dtypes = [BF16, F32, F32] + ([F32] if emit_qf else []) + ([BF16] if emit_vt else [])
    out_specs = [pl.BlockSpec((tm, d), row) for _ in dtypes]
    out_shape = [jax.ShapeDtypeStruct((n, d), dt) for dt in dtypes]
    if emit_vt:
        t = n // batch
        assert tm % tb == 0 and t % tm == 0
        tiles = t // tm
        out_specs.append(pl.BlockSpec((1, ATTN_HEADS, tm // tb, HEAD_W, tb),
                                      lambda i: (i // tiles, 0, i % tiles, 0, 0)))
        out_shape.append(jax.ShapeDtypeStruct((batch, ATTN_HEADS, t // tb, HEAD_W, tb), BF16))
    return pl.pallas_call(
        functools.partial(_attn_pre_kernel, seg=seg, q_scale=q_scale, tb=tb, emit_qf=emit_qf, emit_vt=emit_vt),
        grid=(n // tm,),
        in_specs=[pl.BlockSpec((tm, d), row), pl.BlockSpec((1, d), full),
                  pl.BlockSpec(w_in.shape, full), pl.BlockSpec((1, d), full), pl.BlockSpec((1, d), full)],
        out_specs=out_specs,
        out_shape=out_shape,
        compiler_params=_cparams(("parallel",), 56),
        name="attn_pre",
    )(x, g.reshape(1, d), w_in.astype(BF16),
      jnp.tile(q_norm, reps).reshape(1, d), jnp.tile(k_norm, reps).reshape(1, d))


def _bias_tile_kernel(bucket_ref, rb_ref, o_ref):
    h = pl.program_id(0)
    bucket = bucket_ref[...]
    far = rb_ref[FAR_BUCKET, h]
    acc = jnp.full(bucket.shape, NEG_INF, F32)
    for bk in range(REL_BUCKETS):
        acc = jnp.where(bucket == bk, (rb_ref[bk, h] - far) * LOG2E, acc)
    o_ref[0] = acc


def _bias_tiles(rel_bias, buckets):
    return pl.pallas_call(
        _bias_tile_kernel,
        grid=(ATTN_HEADS,),
        in_specs=[pl.BlockSpec(buckets.shape, lambda h: (0, 0, 0)),
                  pl.BlockSpec(memory_space=pltpu.SMEM)],
        out_specs=pl.BlockSpec((1,) + buckets.shape, lambda h: (h, 0, 0, 0)),
        out_shape=jax.ShapeDtypeStruct((ATTN_HEADS,) + buckets.shape, F32),
        compiler_params=_cparams(("parallel",), 32),
        name="bias_tiles",
    )(jnp.asarray(buckets), rel_bias)


def _block_buckets(tb):
    j = np.arange(tb)[:, None]
    i = np.arange(tb)[None, :]
    diag = np.where(i >= j, _rel_bucket_np(i - j), -1)
    prev = _rel_bucket_np(tb + i - j)
    return np.stack([diag, prev]).astype(np.int32)


def _softmax_group(s_list, vt_list, m_ref, l_ref, acc_ref, col_max=None):
    m_prev = m_ref[...]
    m_new = m_prev
    for i, s in enumerate(s_list):
        m_new = jnp.maximum(m_new, jnp.max(s, axis=0, keepdims=True) if col_max is None else col_max[i])
    alpha = jnp.exp2(m_prev - m_new)
    l_new = alpha * l_ref[...]
    acc = alpha * acc_ref[...]
    for s, vt in zip(s_list, vt_list):
        p = jnp.exp2(s - m_new)
        l_new = l_new + jnp.sum(p, axis=0, keepdims=True)
        acc = acc + _dot(vt, p.astype(BF16))
    m_ref[...] = m_new
    l_ref[...] = l_new
    acc_ref[...] = acc


def _drain_groups(n_groups, qk, softmax, buf_a, buf_b):
    def pair(j, carry):
        g = 2 * j
        qk(g + 1, buf_b)
        softmax(g, buf_a)
        qk(g + 2, buf_a)
        softmax(g + 1, buf_b)
        return carry

    lax.fori_loop(0, jnp.maximum(n_groups - 1, 0) // 2, pair, 0)
    last = n_groups - 1
    odd = jnp.bitwise_and(n_groups, 1) == 1

    @pl.when(odd)
    def _():
        softmax(last, buf_a)

    @pl.when(jnp.logical_and(n_groups > 0, jnp.logical_not(odd)))
    def _():
        qk(last, buf_b)
        softmax(last - 1, buf_a)
        softmax(last, buf_b)


def _diff_attn_kernel(q_ref, k_ref, vt_ref, bias_ref, lq1_ref, lk1_ref, lq2_ref, lk2_ref, subln_ref,
                      o_ref, s_a, s_b, mx_a, mx_b, m1, l1, a1, m2, l2, a2):
    tb = q_ref.shape[1]
    qi = pl.program_id(2)
    nq = pl.num_programs(2)
    q = q_ref[0]
    lane = lax.broadcasted_iota(jnp.int32, q.shape, 1)
    q1 = jnp.where(lane < DIFF_DK, q, jnp.zeros_like(q))
    q2 = jnp.where(lane < DIFF_DK, jnp.zeros_like(q), q)

    for m_ref, l_ref, a_ref in ((m1, l1, a1), (m2, l2, a2)):
        m_ref[...] = jnp.full(m_ref.shape, NEG_INF, F32)
        l_ref[...] = jnp.zeros(l_ref.shape, F32)
        a_ref[...] = jnp.zeros(a_ref.shape, F32)

    def scores(kb):
        kblk = k_ref[0, pl.ds(pl.multiple_of(kb * tb, tb), tb), :]
        return _dot_nt(kblk, q1), _dot_nt(kblk, q2)

    def update(blocks, s_pairs, max_pairs=None):
        vts = [vt_ref[0, 0, kb] for kb in blocks]
        for mp, (m_ref, l_ref, a_ref) in enumerate(((m1, l1, a1), (m2, l2, a2))):
            _softmax_group([s[mp] for s in s_pairs], vts, m_ref, l_ref, a_ref,
                           None if max_pairs is None else [mx[mp] for mx in max_pairs])

    n_far = jnp.maximum(qi - 1, 0)
    n_groups = n_far // DIFF_FAR_GROUP
    prev = jnp.maximum(qi - 1, 0)
    assert DIFF_FAR_GROUP == 2

    def group_blocks(g):
        far = [jnp.clip((g - 1) * DIFF_FAR_GROUP + i, 0, nq - 1) for i in range(DIFF_FAR_GROUP)]
        return [jnp.where(g == 0, near, f) for near, f in zip((qi, prev), far)]

    def store_scores(bufs, i, s_pair):
        buf, mx = bufs
        for mp, s in enumerate(s_pair):
            buf[i, mp] = s
            mx[i, mp] = jnp.max(s, axis=0, keepdims=True)

    def qk_group(g, bufs):
        for i, kb in enumerate(group_blocks(g)):
            store_scores(bufs, i, scores(kb))

    def softmax_group(g, bufs):
        buf, mx = bufs
        update(group_blocks(g), [(buf[i, 0], buf[i, 1]) for i in range(DIFF_FAR_GROUP)],
               [(mx[i, 0], mx[i, 1]) for i in range(DIFF_FAR_GROUP)])

    s_a, s_b = (s_a, mx_a), (s_b, mx_b)
    prev_bias = jnp.where(qi > 0, bias_ref[0, 1], NEG_INF)
    for i, (kb, bias) in enumerate(((qi, bias_ref[0, 0]), (prev, prev_bias))):
        store_scores(s_a, i, [s + bias for s in scores(kb)])
    _drain_groups(1 + n_groups, qk_group, softmax_group, s_a, s_b)

    for i in range(DIFF_FAR_GROUP - 1):
        kb = n_groups * DIFF_FAR_GROUP + i

        @pl.when(kb < n_far)
        def _(kb=kb):
            update([kb], [scores(kb)])

    lam = _lambda_full(lq1_ref, lk1_ref, lq2_ref, lk2_ref)
    o_t = a1[...] * (1.0 / l1[...]) - lam * (a2[...] * (1.0 / l2[...]))
    o = _rms_norm(o_t.T, subln_ref[...]) * (1.0 - DIFF_LAMBDA_INIT)
    o_ref[0] = o.astype(BF16)


def _diff_attn(qb, kb, vt, bias, lq1, lk1, lq2, lk2, subln):
    bsz, t, d = qb.shape
    tb = vt.shape[-1]
    nq = t // tb
    vec = lambda a: a.reshape(1, -1)
    vspec = lambda w: pl.BlockSpec((1, w), lambda b, h, i: (0, 0))
    stat = pltpu.VMEM((1, tb), F32)
    acc = pltpu.VMEM((HEAD_W, tb), F32)
    s_buf = pltpu.VMEM((DIFF_FAR_GROUP, 2, tb, tb), F32)
    mx_buf = pltpu.VMEM((DIFF_FAR_GROUP, 2, 1, tb), F32)
    return pl.pallas_call(
        _diff_attn_kernel,
        grid=(bsz, ATTN_HEADS, nq),
        in_specs=[pl.BlockSpec((1, tb, HEAD_W), lambda b, h, i: (b, i, h)),
                  pl.BlockSpec((1, t, HEAD_W), lambda b, h, i: (b, 0, h)),
                  pl.BlockSpec((1, 1, nq, HEAD_W, tb), lambda b, h, i: (b, h, 0, 0, 0)),
                  pl.BlockSpec((1, 2, tb, tb), lambda b, h, i: (h, 0, 0, 0)),
                  vspec(DIFF_DK), vspec(DIFF_DK), vspec(DIFF_DK), vspec(DIFF_DK), vspec(HEAD_W)],
        out_specs=pl.BlockSpec((1, tb, HEAD_W), lambda b, h, i: (b, i, h)),
        out_shape=jax.ShapeDtypeStruct((bsz, t, d), BF16),
        scratch_shapes=[s_buf, s_buf, mx_buf, mx_buf, stat, stat, acc, stat, stat, acc],
        compiler_params=_cparams(("parallel", "parallel", "arbitrary"), 56),
        name="diff_attn",
    )(qb, kb, vt, bias, vec(lq1), vec(lk1), vec(lq2), vec(lk2), vec(subln))


def _diff_decode_kernel(pt_ref, q_ref, ks_ref, vs_ref, blast_ref, bself_ref,
                        lq1_ref, lk1_ref, lq2_ref, lk2_ref, subln_ref, *rest):
    g = DEC_PAGES_PER_STEP
    k_pages = rest[:g]
    v_pages = rest[g:2 * g]
    o_ref, m1, l1, a1, m2, l2, a2 = rest[2 * g:]
    step = pl.program_id(1)
    last = pl.num_programs(1) - 1

    q = q_ref[0]
    lane = lax.broadcasted_iota(jnp.int32, q.shape, 1)
    q_maps = (jnp.where(lane < DIFF_DK, q, 0.0), jnp.where(lane < DIFF_DK, 0.0, q))
    states = ((m1, l1, a1), (m2, l2, a2))
    rr = lax.broadcasted_iota(jnp.int32, (HEAD_W, 2 * HEAD_W), 0)
    cc = lax.broadcasted_iota(jnp.int32, (HEAD_W, 2 * HEAD_W), 1)
    lane_sum = ((rr < DIFF_DK) == (cc < HEAD_W)).astype(BF16)

    @pl.when(step == 0)
    def _():
        for qm, (m_ref, l_ref, a_ref) in zip(q_maps, states):
            s_self = jnp.sum(qm * ks_ref[0], axis=-1, keepdims=True) + bself_ref[...]
            m_ref[...] = jnp.broadcast_to(s_self, m_ref.shape)
            l_ref[...] = jnp.ones(l_ref.shape, F32)
            a_ref[...] = vs_ref[0]

    tc = DEC_TOKEN_CHUNK
    for i in range(g):
        for c in range(PAGE_SIZE // tc):
            rows = slice(c * tc, (c + 1) * tc)
            k3 = k_pages[i][0, rows]
            v3 = v_pages[i][0, rows]
            prod = (k3 * q).reshape(tc * ATTN_HEADS, HEAD_W)
            sums = _dot(prod.astype(BF16), lane_sum).reshape(tc, ATTN_HEADS, 2 * HEAD_W)
            for mp, (m_ref, l_ref, a_ref) in enumerate(states):
                s = sums[:, :, mp * HEAD_W:(mp + 1) * HEAD_W]
                if i == g - 1:
                    s = s + jnp.where(step == last, blast_ref[rows], 0.0)
                m_prev = m_ref[...]
                m_new = jnp.maximum(m_prev, jnp.max(s, axis=0))
                alpha = jnp.exp2(m_prev - m_new)
                p = jnp.exp2(s - m_new)
                l_ref[...] = alpha * l_ref[...] + jnp.sum(p, axis=0)
                a_ref[...] = alpha * a_ref[...] + jnp.sum(p * v3, axis=0)
                m_ref[...] = m_new

    @pl.when(step == last)
    def _():
        lam = _lambda_full(lq1_ref, lk1_ref, lq2_ref, lk2_ref)
        o = a1[...] * (1.0 / l1[...]) - lam * (a2[...] * (1.0 / l2[...]))
        o_ref[0] = (_rms_norm(o, subln_ref[...]) * (1.0 - DIFF_LAMBDA_INIT)).astype(BF16)


def _decode_bias(rel_bias):
    far = rel_bias[FAR_BUCKET]
    newest = (rel_bias[_rel_bucket_np(PAGE_SIZE - np.arange(PAGE_SIZE))] - far) * LOG2E
    own = (rel_bias[0] - far) * LOG2E
    return newest[:, :, None], own[:, None]


def _diff_decode(page_table, q, k_self, v_self, cache_k, cache_v, rel_bias, lq1, lk1, lq2, lk2, subln):
    n = q.shape[0]
    n_pages = page_table.shape[1]
    g = DEC_PAGES_PER_STEP
    assert n_pages % g == 0
    b_last, b_self = _decode_bias(rel_bias)
    per_head = lambda a: a.reshape(n, ATTN_HEADS, HEAD_W)
    vec = lambda a: a.reshape(1, -1)
    const = lambda shape: pl.BlockSpec(shape, lambda b, j, pt: (0,) * len(shape))
    seq = pl.BlockSpec((1, ATTN_HEADS, HEAD_W), lambda b, j, pt: (b, 0, 0))

    def page_spec(i):
        return pl.BlockSpec((1, PAGE_SIZE, ATTN_HEADS, HEAD_W),
                            lambda b, j, pt: (pt[b * n_pages + j * g + i], 0, 0, 0))

    stat = pltpu.VMEM((ATTN_HEADS, HEAD_W), F32)
    acc = pltpu.VMEM((ATTN_HEADS, HEAD_W), F32)
    out = pl.pallas_call(
        _diff_decode_kernel,
        grid_spec=pltpu.PrefetchScalarGridSpec(
            num_scalar_prefetch=1,
            grid=(n, n_pages // g),
            in_specs=[seq, seq, seq, const((PAGE_SIZE, ATTN_HEADS, 1)), const((ATTN_HEADS, 1)),
                      const((1, DIFF_DK)), const((1, DIFF_DK)), const((1, DIFF_DK)), const((1, DIFF_DK)),
                      const((1, HEAD_W))]
                     + [page_spec(i) for i in range(g)] + [page_spec(i) for i in range(g)],
            out_specs=seq,
            scratch_shapes=[stat, stat, acc, stat, stat, acc]),
        out_shape=jax.ShapeDtypeStruct((n, ATTN_HEADS, HEAD_W), BF16),
        compiler_params=_cparams(("parallel", "arbitrary"), 48),
        name="diff_decode",
    )(page_table.reshape(-1), per_head(q), per_head(k_self), per_head(v_self), b_last, b_self,
      vec(lq1), vec(lk1), vec(lq2), vec(lk2), vec(subln),
      *([cache_k] * g), *([cache_v] * g))
    return out.reshape(n, ATTN_HEADS * HEAD_W)


def _block_mean_kernel(k_ref, o_ref):
    for i in range(o_ref.shape[0]):
        o_ref[i:i + 1, :] = jnp.mean(k_ref[i * MOBA_BLOCK:(i + 1) * MOBA_BLOCK, :], axis=0, keepdims=True)


def _block_mean(k):
    n, d = k.shape
    nb = n // MOBA_BLOCK
    per = _row_tile(nb, 8)
    return pl.pallas_call(
        _block_mean_kernel,
        grid=(nb // per,),
        in_specs=[pl.BlockSpec((per * MOBA_BLOCK, d), lambda i: (i, 0))],
        out_specs=pl.BlockSpec((per, d), lambda i: (i, 0)),
        out_shape=jax.ShapeDtypeStruct((nb, d), F32),
        compiler_params=_cparams(("parallel",), 32),
        name="moba_block_mean",
    )(k)


def _topk_rows(gate, row_idx, k):
    picks = []
    for _ in range(k):
        best = jnp.max(gate, axis=0, keepdims=True)
        first = jnp.min(jnp.where(gate == best, row_idx, gate.shape[0]), axis=0, keepdims=True)
        picks.append(first)
        gate = jnp.where(row_idx == first, NEG_INF, gate)
    return picks


def _moba_attn_kernel(qf_ref, qb_ref, k_ref, vt_ref, kmean_ref, bias_ref, o_ref,
                      sel_scr, s_a, s_b, mx_a, mx_b, m_scr, l_scr, acc_scr):
    tb = qf_ref.shape[1]
    qi = pl.program_id(2)
    q = qb_ref[0]

    gate = _dot_nt(kmean_ref[0], qf_ref[0], precision=HIGHEST)
    blk_idx = lax.broadcasted_iota(jnp.int32, gate.shape, 0)
    past = blk_idx < qi
    picks = _topk_rows(jnp.where(past, gate, NEG_INF), blk_idx, MOBA_TOPK)
    routed = functools.reduce(jnp.logical_or, [blk_idx == p for p in picks])
    sel_scr[...] = (past & routed).astype(F32)

    m_scr[...] = jnp.full(m_scr.shape, NEG_INF, F32)
    l_scr[...] = jnp.zeros(l_scr.shape, F32)
    acc_scr[...] = jnp.zeros(acc_scr.shape, F32)

    nq = pl.num_programs(2)

    def scores(kb, bias, keep):
        kblk = k_ref[0, pl.ds(pl.multiple_of(kb * tb, tb), tb), :]
        s = _dot_nt(kblk, q)
        if bias is not None:
            s = s + bias
        if keep is not None:
            s = jnp.where(keep > 0.0, s, NEG_INF)
        return s

    near = MOBA_FAR_GROUP
    n_far = jnp.maximum(qi - (near - 1), 0)
    n_groups = (n_far + MOBA_FAR_GROUP - 1) // MOBA_FAR_GROUP

    def group_blocks(g):
        out = []
        for i in range(MOBA_FAR_GROUP):
            kb_far = (g - 1) * MOBA_FAR_GROUP + i
            kb = jnp.where(g == 0, qi - i, kb_far)
            valid = jnp.where(g == 0, qi - i >= 0, kb_far < n_far)
            out.append((jnp.clip(kb, 0, nq - 1), valid))
        return out

    def routed(kb, valid):
        return jnp.where(valid, sel_scr[pl.ds(kb, 1), :], 0.0)

    def store_scores(bufs, i, s):
        buf, mx = bufs
        buf[i] = s
        mx[i] = jnp.max(s, axis=0, keepdims=True)

    def qk_group(g, bufs):
        for i, (kb, valid) in enumerate(group_blocks(g)):
            store_scores(bufs, i, scores(kb, None, routed(kb, valid)))

    def softmax_group(g, bufs):
        buf, mx = bufs
        _softmax_group([buf[i] for i in range(MOBA_FAR_GROUP)], [vt_ref[0, 0, kb] for kb, _ in group_blocks(g)],
                       m_scr, l_scr, acc_scr, [mx[i] for i in range(MOBA_FAR_GROUP)])

    s_a, s_b = (s_a, mx_a), (s_b, mx_b)
    for i, (kb, valid) in enumerate(group_blocks(0)):
        bias = bias_ref[0, i] if i < 2 else None
        store_scores(s_a, i, scores(kb, bias, None if i == 0 else routed(kb, valid)))
    _drain_groups(1 + n_groups, qk_group, softmax_group, s_a, s_b)
    o_ref[0] = (acc_scr[...] * (1.0 / l_scr[...])).T.astype(BF16)


def _moba_attn(qf, qb, kb, vt, kmean, bias):
    bsz, t, d = qb.shape
    tb = vt.shape[-1]
    assert tb == MOBA_BLOCK
    nq = t // tb
    nb = kmean.shape[1]
    qspec = pl.BlockSpec((1, tb, HEAD_W), lambda b, h, i: (b, i, h))
    s_buf = pltpu.VMEM((MOBA_FAR_GROUP, tb, tb), F32)
    mx_buf = pltpu.VMEM((MOBA_FAR_GROUP, 1, tb), F32)
    return pl.pallas_call(
        _moba_attn_kernel,
        grid=(bsz, ATTN_HEADS, nq),
        in_specs=[qspec, qspec,
                  pl.BlockSpec((1, t, HEAD_W), lambda b, h, i: (b, 0, h)),
                  pl.BlockSpec((1, 1, nq, HEAD_W, tb), lambda b, h, i: (b, h, 0, 0, 0)),
                  pl.BlockSpec((1, nb, HEAD_W), lambda b, h, i: (b, 0, h)),
                  pl.BlockSpec((1, 2, tb, tb), lambda b, h, i: (h, 0, 0, 0))],
        out_specs=qspec,
        out_shape=jax.ShapeDtypeStruct((bsz, t, d), BF16),
        scratch_shapes=[pltpu.VMEM((nb, tb), F32), s_buf, s_buf, mx_buf, mx_buf,
                        pltpu.VMEM((1, tb), F32), pltpu.VMEM((1, tb), F32),
                        pltpu.VMEM((HEAD_W, tb), F32)],
        compiler_params=_cparams(("parallel", "parallel", "arbitrary"), 32),
        name="moba_attn",
    )(qf, qb, kb, vt, kmean, bias)


def _moba_route_kernel(pt_ref, qf_ref, *rest):
    g = DEC_PAGES_PER_STEP
    k_pages = rest[:g]
    sel_ref, kmean_scr = rest[g:]
    step = pl.program_id(1)
    pages_per_block = MOBA_BLOCK // PAGE_SIZE
    blocks_per_step = g // pages_per_block
    for i in range(blocks_per_step):
        tot = jnp.sum(k_pages[i * pages_per_block][0], axis=0)
        for p in range(1, pages_per_block):
            tot = tot + jnp.sum(k_pages[i * pages_per_block + p][0], axis=0)
        kmean_scr[step * blocks_per_step + i] = tot * (1.0 / MOBA_BLOCK)

    @pl.when(step == pl.num_programs(1) - 1)
    def _():
        gate = jnp.sum(kmean_scr[...] * qf_ref[0], axis=-1, keepdims=True)
        blk_idx = lax.broadcasted_iota(jnp.int32, gate.shape, 0)
        for t, chosen in enumerate(_topk_rows(gate, blk_idx, MOBA_TOPK)):
            sel_ref[0, t] = jnp.broadcast_to(chosen[0], sel_ref.shape[2:])


def _moba_route(page_table, qf, cache_k):
    n = qf.shape[0]
    n_pages = page_table.shape[1]
    g = DEC_PAGES_PER_STEP
    nb = n_pages * PAGE_SIZE // MOBA_BLOCK
    assert n_pages % g == 0 and g % (MOBA_BLOCK // PAGE_SIZE) == 0 and nb >= MOBA_TOPK

    def page_spec(i):
        return pl.BlockSpec((1, PAGE_SIZE, ATTN_HEADS, HEAD_W),
                            lambda b, j, pt: (pt[b * n_pages + j * g + i], 0, 0, 0))

    return pl.pallas_call(
        _moba_route_kernel,
        grid_spec=pltpu.PrefetchScalarGridSpec(
            num_scalar_prefetch=1,
            grid=(n, n_pages // g),
            in_specs=[pl.BlockSpec((1, ATTN_HEADS, HEAD_W), lambda b, j, pt: (b, 0, 0))]
                     + [page_spec(i) for i in range(g)],
            out_specs=pl.BlockSpec((1, MOBA_TOPK, ATTN_HEADS, HEAD_W), lambda b, j, pt: (b, 0, 0, 0)),
            scratch_shapes=[pltpu.VMEM((nb, ATTN_HEADS, HEAD_W), F32)]),
        out_shape=jax.ShapeDtypeStruct((n, MOBA_TOPK, ATTN_HEADS, HEAD_W), jnp.int32),
        compiler_params=_cparams(("parallel", "arbitrary"), 32),
        name="moba_route",
    )(page_table.reshape(-1), qf.reshape(n, ATTN_HEADS, HEAD_W), *([cache_k] * g))


def _moba_decode_kernel(pt_ref, sel_ref, q_ref, ks_ref, vs_ref, bias_ref, ck_ref, cv_ref, o_ref,
                        k_buf, v_buf, sems, *, n_pages, newest_block):
    pages_per_block = MOBA_BLOCK // PAGE_SIZE
    n_sel = MOBA_TOPK * pages_per_block
    b = pl.program_id(0)

    def block_of(h, t):
        return sel_ref[(b * MOBA_TOPK + t) * ATTN_HEADS + h]

    def copies(h, i):
        page = pt_ref[b * n_pages + block_of(h, i // pages_per_block) * pages_per_block + i % pages_per_block]
        return (pltpu.make_async_copy(ck_ref.at[page, :, h, :], k_buf.at[h, i], sems.at[0, h, i]),
                pltpu.make_async_copy(cv_ref.at[page, :, h, :], v_buf.at[h, i], sems.at[1, h, i]))

    for h in range(ATTN_HEADS):
        for i in range(n_sel):
            for cp in copies(h, i):
                cp.start()

    for h in range(ATTN_HEADS):
        for i in range(n_sel):
            for cp in copies(h, i):
                cp.wait()

    for h in range(ATTN_HEADS):
        q = q_ref[0, h:h + 1, :]
        q8 = jnp.broadcast_to(q, (8, HEAD_W)).astype(BF16)
        s_self = jnp.sum(q * ks_ref[0, h:h + 1, :], axis=-1, keepdims=True) + bias_ref[h, 1:2, 0:1]
        k_all = k_buf[h].reshape(n_sel * PAGE_SIZE, HEAD_W).astype(BF16)
        v_all = v_buf[h].reshape(n_sel * PAGE_SIZE, HEAD_W).astype(BF16)
        near = [jnp.where((i % pages_per_block == pages_per_block - 1) & (block_of(h, i // pages_per_block) == newest_block),
                          bias_ref[h, 0:1, :], 0.0) for i in range(n_sel)]
        s = _dot_nt(q8, k_all)[0:1] + jnp.concatenate(near, axis=-1)
        m = jnp.maximum(s_self, jnp.max(s, axis=-1, keepdims=True))
        p_self = jnp.exp2(s_self - m)
        p = jnp.exp2(s - m)
        l = p_self + jnp.sum(p, axis=-1, keepdims=True)
        o = p_self * vs_ref[0, h:h + 1, :] + _dot(jnp.broadcast_to(p, (8, p.shape[1])).astype(BF16), v_all)[0:1]
        o_ref[0, h:h + 1, :] = (o * (1.0 / l)).astype(BF16)


def _moba_decode(page_table, sel, q, k_self, v_self, cache_k, cache_v, rel_bias):
    n = q.shape[0]
    n_pages = page_table.shape[1]
    pages_per_block = MOBA_BLOCK // PAGE_SIZE
    n_sel = MOBA_TOPK * pages_per_block
    b_last, b_self = _decode_bias(rel_bias)
    bias = jnp.stack([b_last[:, :, 0].T, jnp.broadcast_to(b_self, (ATTN_HEADS, PAGE_SIZE))], axis=1)
    per_head = lambda a: a.reshape(n, ATTN_HEADS, HEAD_W)
    seq = pl.BlockSpec((1, ATTN_HEADS, HEAD_W), lambda b, pt, sl: (b, 0, 0))
    page_buf = pltpu.VMEM((ATTN_HEADS, n_sel, PAGE_SIZE, HEAD_W), F32)
    out = pl.pallas_call(
        functools.partial(_moba_decode_kernel, n_pages=n_pages, newest_block=n_pages // pages_per_block - 1),
        grid_spec=pltpu.PrefetchScalarGridSpec(
            num_scalar_prefetch=2,
            grid=(n,),
            in_specs=[seq, seq, seq, pl.BlockSpec(bias.shape, lambda b, pt, sl: (0, 0, 0)),
                      pl.BlockSpec(memory_space=pl.ANY), pl.BlockSpec(memory_space=pl.ANY)],
            out_specs=seq,
            scratch_shapes=[page_buf, page_buf, pltpu.SemaphoreType.DMA((2, ATTN_HEADS, n_sel))]),
        out_shape=jax.ShapeDtypeStruct((n, ATTN_HEADS, HEAD_W), BF16),
        compiler_params=_cparams(("arbitrary",), 32),
        name="moba_decode",
    )(page_table.reshape(-1), sel[:, :, :, 0].reshape(-1),
      per_head(q), per_head(k_self), per_head(v_self), bias, cache_k, cache_v)
    return out.reshape(n, ATTN_HEADS * HEAD_W)


def kernel(x_prompt, x_sample, state_gla_l0, cache_k_l1, cache_v_l1, cache_k_l2, cache_v_l2, state_gla_l3, page_table, rel_bias, norm_mix, norm_mlp, mlp_w_up, mlp_w_down, gla_w_in, gla_w_a1, gla_w_a2, gla_b_a, gla_norm, gla_w_out, diff_w_in, diff_q_norm, diff_k_norm, diff_lq1, diff_lk1, diff_lq2, diff_lk2, diff_subln, diff_w_out, moba_w_in, moba_q_norm, moba_k_norm, moba_w_out):
    bsz, t, d = x_prompt.shape
    n_dec = x_sample.shape[0]
    assert d == D_MODEL and x_sample.shape[1] == 1
    xp = x_prompt.reshape(bsz * t, d)
    xs = x_sample.reshape(n_dec, d)
    seq = lambda a: a.reshape(bsz, t, a.shape[-1])
    heads = lambda a: a.reshape(a.shape[:-1] + (ATTN_HEADS, HEAD_W))

    def mlp(x, mo, w_out, li):
        return _post(x, mo, w_out, norm_mlp[li], mlp_w_up[li], mlp_w_down[li])

    def gla_layer(xp, xs, li, gi, state_dec):
        w = (gla_w_in[gi], gla_w_a1[gi], gla_w_a2[gi], gla_b_a[gi])
        q, k, v, r, la = _gla_pre(xp, norm_mix[li], *w)
        s0 = jnp.zeros((bsz,) + state_dec.shape[1:], F32)
        mo_p, st_p = _gla_scan(seq(q), seq(k), seq(la), seq(v), seq(r), gla_norm[gi], s0)
        q, k, v, r, la = _gla_pre(xs, norm_mix[li], *w)
        mo_s, st_s = _gla_step(q, k, la, v, r, gla_norm[gi], state_dec)
        xp = mlp(xp, mo_p.reshape(bsz * t, d), gla_w_out[gi], li)
        xs = mlp(xs, mo_s, gla_w_out[gi], li)
        return xp, xs, st_p, st_s

    xp, xs, st_p0, st_s0 = gla_layer(xp, xs, 0, 0, state_gla_l0)

    tb = min(DIFF_TB, t)
    lam_w = (diff_lq1, diff_lk1, diff_lq2, diff_lk2, diff_subln)
    pre = dict(seg=DIFF_DK, q_scale=DIFF_DK ** -0.5 * LOG2E)
    qb, k, v, kb, vt = _attn_pre(xp, norm_mix[1], diff_w_in, diff_q_norm, diff_k_norm, batch=bsz, tb=tb, **pre)
    bias = _bias_tiles(rel_bias, _block_buckets(tb))
    mo_p = _diff_attn(seq(qb), seq(kb), vt, bias, *lam_w)
    k_p1, v_p1 = heads(seq(k)), heads(seq(v))
    qb, k, v = _attn_pre(xs, norm_mix[1], diff_w_in, diff_q_norm, diff_k_norm, **pre)
    mo_s = _diff_decode(page_table, qb.astype(F32), k, v, cache_k_l1, cache_v_l1, rel_bias, *lam_w)
    k_s1, v_s1 = heads(k.reshape(n_dec, 1, d)), heads(v.reshape(n_dec, 1, d))
    xp = mlp(xp, mo_p.reshape(bsz * t, d), diff_w_out, 1)
    xs = mlp(xs, mo_s, diff_w_out, 1)

    pre = dict(seg=HEAD_W, q_scale=HEAD_W ** -0.5 * LOG2E, emit_qf=True)
    qb, k, v, qf, kb, vt = _attn_pre(xp, norm_mix[2], moba_w_in, moba_q_norm, moba_k_norm, batch=bsz, tb=MOBA_TB, **pre)
    kmean = _block_mean(k).reshape(bsz, t // MOBA_BLOCK, d)
    bias = _bias_tiles(rel_bias, _block_buckets(MOBA_TB))
    mo_p = _moba_attn(seq(qf), seq(qb), seq(kb), vt, kmean, bias)
    k_p2, v_p2 = heads(seq(k)), heads(seq(v))
    qb, k, v, qf = _attn_pre(xs, norm_mix[2], moba_w_in, moba_q_norm, moba_k_norm, **pre)
    sel = _moba_route(page_table, qf, cache_k_l2)
    mo_s = _moba_decode(page_table, sel, qb.astype(F32), k, v, cache_k_l2, cache_v_l2, rel_bias)
    k_s2, v_s2 = heads(k.reshape(n_dec, 1, d)), heads(v.reshape(n_dec, 1, d))
    xp = mlp(xp, mo_p.reshape(bsz * t, d), moba_w_out, 2)
    xs = mlp(xs, mo_s, moba_w_out, 2)

    xp, xs, st_p3, st_s3 = gla_layer(xp, xs, 3, 1, state_gla_l3)

    return (xp.reshape(bsz, t, d), xs.reshape(n_dec, 1, d), st_p0, st_s0,
            k_p1, v_p1, k_s1, v_s1, k_p2, v_p2, k_s2, v_s2, st_p3, st_s3)
```
